```python
import jax, jax.numpy as jnp
from jax import lax
import numpy as np

D_MODEL = 2048
BATCH = 4
SEQ = 4096
DEPTH = 1
DEC_BATCH = 2
DEC_SEQ = 16384
PAST_LEN = 128

HEAD_DIM = 128
N_Q_HEADS = 16
N_KV_HEADS = 4
Q_PER_KV = N_Q_HEADS // N_KV_HEADS
ATTN_WIDTH = N_Q_HEADS * HEAD_DIM
KV_WIDTH = N_KV_HEADS * HEAD_DIM
WINDOW = 128
BLOCK = 128
CHUNK = 128
N_SGU_GROUPS = 16
SGU_WIDTH = D_MODEL
SGU_GROUP_DIM = SGU_WIDTH // N_SGU_GROUPS
N_EXPERTS = 16
CAPACITY_FACTOR = 2
D_FF_EXPERT = 2816
NORM_EPS = 1e-6
IN_SPLITS = (ATTN_WIDTH, KV_WIDTH, KV_WIDTH, SGU_WIDTH, SGU_WIDTH, D_MODEL, D_MODEL)
IN_WIDTH = sum(IN_SPLITS)

kernel_name = "hybrid_sgu_swa_ec_moe_encoder"


def rmsnorm(x, g):
    xf = x.astype(jnp.float32)
    y = xf * lax.rsqrt(jnp.mean(xf * xf, axis=-1, keepdims=True) + NORM_EPS)
    return (y * g.astype(jnp.float32)).astype(x.dtype)


def layernorm(x, g, b):
    xf = x.astype(jnp.float32)
    mu = jnp.mean(xf, axis=-1, keepdims=True)
    var = jnp.mean(jnp.square(xf - mu), axis=-1, keepdims=True)
    return (xf - mu) * lax.rsqrt(var + NORM_EPS) * g.astype(jnp.float32) + b.astype(jnp.float32)


def alibi_slopes(n):
    return jnp.exp2(-8.0 * jnp.arange(1, n + 1, dtype=jnp.float32) / n)


def windowed_attention(q, k, v, sink):
    B, S = q.shape[0], q.shape[1]
    nb = S // BLOCK
    qb = q.reshape(B, nb, BLOCK, N_KV_HEADS, Q_PER_KV, HEAD_DIM)
    pad = ((0, 0), (BLOCK, BLOCK), (0, 0), (0, 0))
    kp = jnp.pad(k, pad).reshape(B, nb + 2, BLOCK, N_KV_HEADS, HEAD_DIM)
    vp = jnp.pad(v, pad).reshape(B, nb + 2, BLOCK, N_KV_HEADS, HEAD_DIM)
    kb = jnp.concatenate([kp[:, :-2], kp[:, 1:-1], kp[:, 2:]], axis=2)
    vb = jnp.concatenate([vp[:, :-2], vp[:, 1:-1], vp[:, 2:]], axis=2)
    s = jnp.einsum('bnqhgd,bnkhd->bnhgqk', qb, kb, preferred_element_type=jnp.float32)
    qi = jnp.arange(BLOCK)[:, None]
    kj = jnp.arange(3 * BLOCK)[None, :]
    rel = kj - BLOCK - qi
    kpos = jnp.arange(nb)[:, None] * BLOCK - BLOCK + jnp.arange(3 * BLOCK)[None, :]
    valid = (jnp.abs(rel) <= WINDOW)[None] & ((kpos >= 0) & (kpos < S))[:, None, :]
    slopes = alibi_slopes(N_Q_HEADS).reshape(N_KV_HEADS, Q_PER_KV)
    bias = -slopes[:, :, None, None] * jnp.abs(rel).astype(jnp.float32)[None, None]
    s = jnp.where(valid[None, :, None, None], s + bias[None, None], -jnp.inf)
    sink_hg = sink.astype(jnp.float32).reshape(N_KV_HEADS, Q_PER_KV)[None, None, :, :, None]
    m = jnp.maximum(jnp.max(s, axis=-1), sink_hg)
    p = jnp.exp(s - m[..., None])
    denom = jnp.sum(p, axis=-1) + jnp.exp(sink_hg - m)
    p = p / denom[..., None]
    o = jnp.einsum('bnhgqk,bnkhd->bnqhgd', p.astype(v.dtype), vb)
    return o.reshape(B, S, ATTN_WIDTH)


def spatial_gating(u, v, ln_g, ln_b, w_s, b_s):
    B, S = u.shape[0], u.shape[1]
    vn = layernorm(v, ln_g, ln_b).reshape(B, S // CHUNK, CHUNK, N_SGU_GROUPS, SGU_GROUP_DIM)
    mixed = jnp.einsum('gts,bnsgc->bntgc', w_s.astype(jnp.float32), vn)
    mixed = mixed + b_s.astype(jnp.float32).T[None, None, :, :, None]
    return u * mixed.reshape(B, S, SGU_WIDTH).astype(u.dtype)


def token_mixer(h, w_in, q_norm_g, k_norm_g, attn_sink, sgu_ln_g, sgu_ln_b, sgu_w, sgu_b,
                w_o_attn, w_o_sgu, w_out):
    B, S, _ = h.shape
    z = h @ w_in
    idx = list(np.cumsum(IN_SPLITS)[:-1])
    q, k, v, u, vg, ga, gs = jnp.split(z, idx, axis=-1)
    q = rmsnorm(q.reshape(B, S, N_Q_HEADS, HEAD_DIM), q_norm_g) * (HEAD_DIM ** -0.5)
    k = rmsnorm(k.reshape(B, S, N_KV_HEADS, HEAD_DIM), k_norm_g)
    v = v.reshape(B, S, N_KV_HEADS, HEAD_DIM)
    attn = windowed_attention(q, k, v, attn_sink) @ w_o_attn
    sgu = spatial_gating(jax.nn.gelu(u), jax.nn.gelu(vg), sgu_ln_g, sgu_ln_b, sgu_w, sgu_b) @ w_o_sgu
    merged = jax.nn.sigmoid(ga) * attn + jax.nn.sigmoid(gs) * sgu
    return merged @ w_out


def expert_choice_ffn(h, w_router, w_gate, w_up, w_down):
    B, S, D = h.shape
    T = B * S
    C = CAPACITY_FACTOR * T // N_EXPERTS
    xt = h.reshape(T, D)
    aff = jax.nn.softmax((xt @ w_router).astype(jnp.float32), axis=-1)
    top_w, top_idx = lax.top_k(aff.T, C)
    xe = xt[top_idx]
    g = jnp.einsum('ecd,edf->ecf', xe, w_gate)
    up = jnp.einsum('ecd,edf->ecf', xe, w_up)
    y = jnp.einsum('ecf,efd->ecd', jax.nn.silu(g) * up, w_down)
    y = y * top_w[..., None].astype(y.dtype)
    out = jnp.zeros((T, D), y.dtype).at[top_idx.reshape(-1)].add(y.reshape(-1, D))
    return out.reshape(B, S, D)


def trunk(x, c, weights):
    (w_ada, b_ada, norm1_g, norm2_g, w_in, q_norm_g, k_norm_g, attn_sink, sgu_ln_g, sgu_ln_b,
     sgu_w, sgu_b, w_o_attn, w_o_sgu, w_out, w_router, w_gate, w_up, w_down) = weights
    for l in range(DEPTH):
        mod = jax.nn.silu(c) @ w_ada[l] + b_ada[l]
        sh1, sc1, g1, sh2, sc2, g2 = jnp.split(mod[:, None, :], 6, axis=-1)
        h = rmsnorm(x, norm1_g[l]) * (1 + sc1) + sh1
        x = x + g1 * token_mixer(h, w_in[l], q_norm_g[l], k_norm_g[l], attn_sink[l], sgu_ln_g[l],
                                 sgu_ln_b[l], sgu_w[l], sgu_b[l], w_o_attn[l], w_o_sgu[l], w_out[l])
        h = rmsnorm(x, norm2_g[l]) * (1 + sc2) + sh2
        x = x + g2 * expert_choice_ffn(h, w_router[l], w_gate[l], w_up[l], w_down[l])
    return x


def setup_inputs(seed: int = 0) -> dict:
    key = jax.random.key(seed)
    ks = jax.random.split(key, 24)
    f32 = jnp.float32
    nrm = lambda k, shape, s: jax.random.normal(k, shape, f32) * s
    L, D = DEPTH, D_MODEL
    return {
        "x_prompt": nrm(ks[0], (BATCH, SEQ, D), 1.0),
        "x_sample": nrm(ks[1], (DEC_BATCH, DEC_SEQ, D), 1.0),
        "c_prompt": nrm(ks[2], (BATCH, D), 1.0),
        "c_sample": nrm(ks[3], (DEC_BATCH, D), 1.0),
        "w_ada": nrm(ks[4], (L, D, 6 * D), 0.3 * D ** -0.5),
        "b_ada": nrm(ks[5], (L, 6 * D), 0.02),
        "norm1_g": 1.0 + nrm(ks[6], (L, D), 0.02),
        "norm2_g": 1.0 + nrm(ks[7], (L, D), 0.02),
        "w_in": nrm(ks[8], (L, D, IN_WIDTH), D ** -0.5),
        "q_norm_g": 1.0 + nrm(ks[9], (L, HEAD_DIM), 0.02),
        "k_norm_g": 1.0 + nrm(ks[10], (L, HEAD_DIM), 0.02),
        "attn_sink": nrm(ks[11], (L, N_Q_HEADS), 0.5),
        "sgu_ln_g": 1.0 + nrm(ks[12], (L, SGU_WIDTH), 0.02),
        "sgu_ln_b": nrm(ks[13], (L, SGU_WIDTH), 0.02),
        "sgu_w": nrm(ks[14], (L, N_SGU_GROUPS, CHUNK, CHUNK), CHUNK ** -0.5),
        "sgu_b": 1.0 + nrm(ks[15], (L, N_SGU_GROUPS, CHUNK), 0.02),
        "w_o_attn": nrm(ks[16], (L, ATTN_WIDTH, D), ATTN_WIDTH ** -0.5),
        "w_o_sgu": nrm(ks[17], (L, SGU_WIDTH, D), SGU_WIDTH ** -0.5),
        "w_out": nrm(ks[18], (L, D, D), D ** -0.5),
        "w_router": nrm(ks[19], (L, D, N_EXPERTS), D ** -0.5),
        "w_gate": nrm(ks[20], (L, N_EXPERTS, D, D_FF_EXPERT), D ** -0.5),
        "w_up": nrm(ks[21], (L, N_EXPERTS, D, D_FF_EXPERT), D ** -0.5),
        "w_down": nrm(ks[22], (L, N_EXPERTS, D_FF_EXPERT, D), D_FF_EXPERT ** -0.5),
    }


def reference(x_prompt, x_sample, c_prompt, c_sample, w_ada, b_ada, norm1_g, norm2_g, w_in,
              q_norm_g, k_norm_g, attn_sink, sgu_ln_g, sgu_ln_b, sgu_w, sgu_b, w_o_attn, w_o_sgu,
              w_out, w_router, w_gate, w_up, w_down):
    weights = (w_ada, b_ada, norm1_g, norm2_g, w_in, q_norm_g, k_norm_g, attn_sink, sgu_ln_g,
               sgu_ln_b, sgu_w, sgu_b, w_o_attn, w_o_sgu, w_out, w_router, w_gate, w_up, w_down)
    y_prompt = trunk(x_prompt, c_prompt, weights)
    y_sample = trunk(x_sample, c_sample, weights)
    return (y_prompt, y_sample)
```

```python
import functools

import jax
import jax.numpy as jnp
from jax import lax
from jax.experimental import pallas as pl
from jax.experimental.pallas import tpu as pltpu

F32 = jnp.float32
BF16 = jnp.bfloat16

D_MODEL = 2048
HEAD_DIM = 128
N_Q_HEADS = 16
N_KV_HEADS = 4
Q_PER_KV = N_Q_HEADS // N_KV_HEADS
KV_WIDTH = N_KV_HEADS * HEAD_DIM
BLOCK = 128
N_SGU_GROUPS = 16
N_EXPERTS = 16
CAPACITY_FACTOR = 2
D_FF = 2816
NORM_EPS = 1e-6
IN_WIDTH = 11264
N_SEQ_PAD = 8

COL_TILE = 1024
Q_TILES = (0, 2)
GELU_TILES = (2, 6)
SIG_TILES = (6, 10)
KV_TILE = 10
N_COL_TILES = IN_WIDTH // COL_TILE

VMEM_LIMIT = 56 * 1024 * 1024


def _cparams(sem):
    return pltpu.CompilerParams(dimension_semantics=sem, vmem_limit_bytes=VMEM_LIMIT)


def _gelu_tanh(x):
    c = 0.7978845608028654
    return 0.5 * x * (1.0 + jnp.tanh(c * (x + 0.044715 * (x * x * x))))


def _sigmoid(x):
    return 1.0 / (1.0 + jnp.exp(-x))


def _mod_kernel(c_ref, w_ref, b_ref, o_ref):
    c = c_ref[...]
    s = c * _sigmoid(c)
    o_ref[...] = jnp.dot(s, w_ref[...], precision=lax.Precision.HIGHEST,
                         preferred_element_type=F32) + b_ref[...]


def _modulation(c_pad, w_ada, b_ada):
    n = w_ada.shape[1]
    tn = 1024
    return pl.pallas_call(
        _mod_kernel,
        grid=(n // tn,),
        in_specs=[pl.BlockSpec((N_SEQ_PAD, D_MODEL), lambda j: (0, 0)),
                  pl.BlockSpec((D_MODEL, tn), lambda j: (0, j)),
                  pl.BlockSpec((1, tn), lambda j: (0, j))],
        out_specs=pl.BlockSpec((N_SEQ_PAD, tn), lambda j: (0, j)),
        out_shape=jax.ShapeDtypeStruct((N_SEQ_PAD, n), F32),
        compiler_params=_cparams(("arbitrary",)),
        name="adaln_mod",
    )(c_pad, w_ada, b_ada.reshape(1, n))


def _inproj_kernel(x_ref, sh_ref, sc_ref, g_ref, w_ref, qg_ref, kg_ref, o_ref, h_ref, *, row_chunk):
    j = pl.program_id(1)
    tm = x_ref.shape[0]

    @pl.when(j == 0)
    def _():
        x = x_ref[...]
        ms = jnp.mean(x * x, axis=-1, keepdims=True)
        y = x * lax.rsqrt(ms + NORM_EPS) * g_ref[...]
        h_ref[...] = (y * (1.0 + sc_ref[...]) + sh_ref[...]).astype(BF16)

    def head_norm(sub, g):
        ms = jnp.mean(sub * sub, axis=-1, keepdims=True)
        return sub * lax.rsqrt(ms + NORM_EPS) * g

    for r in range(tm // row_chunk):
        rows = pl.ds(r * row_chunk, row_chunk)
        acc = jnp.dot(h_ref[rows, :], w_ref[...], preferred_element_type=F32)

        @pl.when(j < Q_TILES[1])
        def _():
            g = qg_ref[...] * (HEAD_DIM ** -0.5)
            for hh in range(COL_TILE // HEAD_DIM):
                cols = slice(hh * HEAD_DIM, (hh + 1) * HEAD_DIM)
                o_ref[rows, cols] = head_norm(acc[:, cols], g).astype(BF16)

        @pl.when((j >= GELU_TILES[0]) & (j < GELU_TILES[1]))
        def _():
            o_ref[rows, :] = _gelu_tanh(acc).astype(BF16)

        @pl.when((j >= SIG_TILES[0]) & (j < SIG_TILES[1]))
        def _():
            o_ref[rows, :] = _sigmoid(acc).astype(BF16)

        @pl.when(j == KV_TILE)
        def _():
            g = kg_ref[...]
            for hh in range(N_KV_HEADS):
                cols = slice(hh * HEAD_DIM, (hh + 1) * HEAD_DIM)
                o_ref[rows, cols] = head_norm(acc[:, cols], g).astype(BF16)
            o_ref[rows, KV_WIDTH:] = acc[:, KV_WIDTH:].astype(BF16)


def _in_projection(x2, mod3, norm_g, w_in_r, q_g, k_g, seq_len, tm):
    t = x2.shape[0]
    seq = lambda i: (i * tm) // seq_len
    return pl.pallas_call(
        functools.partial(_inproj_kernel, row_chunk=256),
        grid=(t // tm, N_COL_TILES),
        in_specs=[pl.BlockSpec((tm, D_MODEL), lambda i, j: (i, 0)),
                  pl.BlockSpec((None, 1, D_MODEL), lambda i, j: (seq(i), 0, 0)),
                  pl.BlockSpec((None, 1, D_MODEL), lambda i, j: (seq(i), 0, 1)),
                  pl.BlockSpec((1, D_MODEL), lambda i, j: (0, 0)),
                  pl.BlockSpec((D_MODEL, COL_TILE), lambda i, j: (0, j)),
                  pl.BlockSpec((1, HEAD_DIM), lambda i, j: (0, 0)),
                  pl.BlockSpec((1, HEAD_DIM), lambda i, j: (0, 0))],
        out_specs=pl.BlockSpec((tm, COL_TILE), lambda i, j: (i, j)),
        out_shape=jax.ShapeDtypeStruct((t, IN_WIDTH), BF16),
        scratch_shapes=[pltpu.VMEM((tm, D_MODEL), BF16)],
        compiler_params=_cparams(("arbitrary", "arbitrary")),
        name="norm_inproj",
    )(x2, mod3, mod3, norm_g, w_in_r, q_g, k_g)


def _alibi_slope(head):
    return 2.0 ** (-8.0 * (head + 1) / N_Q_HEADS)


def _attn_kernel(sink_ref, q_ref, kp_ref, kc_ref, kn_ref, vp_ref, vc_ref, vn_ref, o_ref, *, blocks_per_seq):
    n = pl.program_id(0)
    pos = n % blocks_per_seq
    first = pos == 0
    last = pos == blocks_per_seq - 1

    qi = lax.broadcasted_iota(jnp.int32, (BLOCK, 3 * BLOCK), 0)
    kj = lax.broadcasted_iota(jnp.int32, (BLOCK, 3 * BLOCK), 1)
    rel = kj - BLOCK - qi
    absrel = jnp.abs(rel)
    valid = absrel <= BLOCK
    valid = valid & jnp.logical_not(first & (kj < BLOCK)) & jnp.logical_not(last & (kj >= 2 * BLOCK))
    absrel_f = absrel.astype(F32)
    neg_inf = jnp.float32(-jnp.inf)

    for h in range(N_KV_HEADS):
        kv_cols = slice(h * HEAD_DIM, (h + 1) * HEAD_DIM)
        kcat = jnp.concatenate([kp_ref[:, kv_cols], kc_ref[:, kv_cols], kn_ref[:, kv_cols]], axis=0)
        vcat = jnp.concatenate([vp_ref[:, kv_cols], vc_ref[:, kv_cols], vn_ref[:, kv_cols]], axis=0)
        for g in range(Q_PER_KV):
            head = h * Q_PER_KV + g
            cols = slice(head * HEAD_DIM, (head + 1) * HEAD_DIM)
            s = lax.dot_general(q_ref[:, cols], kcat, (((1,), (1,)), ((), ())),
                                preferred_element_type=F32)
            s = jnp.where(valid, s - _alibi_slope(head) * absrel_f, neg_inf)
            sink = sink_ref[head]
            m = jnp.maximum(jnp.max(s, axis=-1, keepdims=True), sink)
            p = jnp.exp(s - m)
            denom = jnp.sum(p, axis=-1, keepdims=True) + jnp.exp(sink - m)
            o = jnp.dot(p.astype(BF16), vcat, preferred_element_type=F32)
            o_ref[:, cols] = (o / denom).astype(BF16)


def _attention(z, sink, seq_len):
    t = z.shape[0]
    nb = t // BLOCK
    bps = seq_len // BLOCK
    k_col = (IN_WIDTH - 2 * KV_WIDTH) // KV_WIDTH
    v_col = k_col + 1
    prev = lambda n, s: (jnp.maximum(n - 1, 0), k_col)
    cur = lambda n, s: (n, k_col)
    nxt = lambda n, s: (jnp.minimum(n + 1, nb - 1), k_col)
    vprev = lambda n, s: (jnp.maximum(n - 1, 0), v_col)
    vcur = lambda n, s: (n, v_col)
    vnxt = lambda n, s: (jnp.minimum(n + 1, nb - 1), v_col)
    kv_spec = lambda f: pl.BlockSpec((BLOCK, KV_WIDTH), f)
    grid_spec = pltpu.PrefetchScalarGridSpec(
        num_scalar_prefetch=1,
        grid=(nb,),
        in_specs=[pl.BlockSpec((BLOCK, D_MODEL), lambda n, s: (n, 0)),
                  kv_spec(prev), kv_spec(cur), kv_spec(nxt),
                  kv_spec(vprev), kv_spec(vcur), kv_spec(vnxt)],
        out_specs=pl.BlockSpec((BLOCK, D_MODEL), lambda n, s: (n, 0)),
    )
    return pl.pallas_call(
        functools.partial(_attn_kernel, blocks_per_seq=bps),
        grid_spec=grid_spec,
        out_shape=jax.ShapeDtypeStruct((t, D_MODEL), BF16),
        compiler_params=_cparams(("arbitrary",)),
        name="window_attn",
    )(sink, z, z, z, z, z, z, z)


def _post_kernel(attn_ref, u_ref, vg_ref, ga_ref, gs_ref, x_ref, g1_ref, sh2_ref, sc2_ref, n2g_ref,
                 lng_ref, lnb_ref, ws_ref, bs_ref, woa_ref, wos_ref, wout_ref, wr_ref,
                 x1_ref, h2_ref, aff_ref, sgu_ref):
    tm = x_ref.shape[0]
    vg = vg_ref[...].astype(F32)
    mu = jnp.mean(vg, axis=-1, keepdims=True)
    cen = vg - mu
    var = jnp.mean(cen * cen, axis=-1, keepdims=True)
    vn = (cen * lax.rsqrt(var + NORM_EPS) * lng_ref[...] + lnb_ref[...]).astype(BF16)
    for c in range(tm // BLOCK):
        rows = slice(c * BLOCK, (c + 1) * BLOCK)
        for g in range(N_SGU_GROUPS):
            cols = slice(g * BLOCK, (g + 1) * BLOCK)
            mixed = jnp.dot(ws_ref[g], vn[rows, cols], preferred_element_type=F32) + bs_ref[g]
            sgu_ref[rows, cols] = (u_ref[rows, cols].astype(F32) * mixed).astype(BF16)
    a = jnp.dot(attn_ref[...], woa_ref[...], preferred_element_type=F32)
    s = jnp.dot(sgu_ref[...], wos_ref[...], preferred_element_type=F32)
    merged = (ga_ref[...].astype(F32) * a + gs_ref[...].astype(F32) * s).astype(BF16)
    mix = jnp.dot(merged, wout_ref[...], preferred_element_type=F32)
    x1 = x_ref[...] + g1_ref[...] * mix
    x1_ref[...] = x1
    ms = jnp.mean(x1 * x1, axis=-1, keepdims=True)
    h2 = x1 * lax.rsqrt(ms + NORM_EPS) * n2g_ref[...]
    h2 = h2 * (1.0 + sc2_ref[...]) + sh2_ref[...]
    h2_ref[...] = h2.astype(BF16)
    logits = jnp.dot(h2, wr_ref[...], precision=lax.Precision.HIGHEST, preferred_element_type=F32)
    logits = logits - jnp.max(logits, axis=-1, keepdims=True)
    e = jnp.exp(logits)
    aff_ref[...] = e / jnp.sum(e, axis=-1, keepdims=True)


def _post_mixer(attn_o, z, x2, mod3, norm2_g, ln_g, ln_b, ws, bs, w_oa, w_os, w_out, w_r, seq_len, tm):
    t = x2.shape[0]
    seq = lambda i: (i * tm) // seq_len
    const2 = lambda i: (0, 0)
    const3 = lambda i: (0, 0, 0)
    resident = lambda shape, imap: pl.BlockSpec(shape, imap, pipeline_mode=pl.Buffered(1))
    tok = lambda col: pl.BlockSpec((tm, D_MODEL), lambda i: (i, col))
    modv = lambda col: pl.BlockSpec((None, 1, D_MODEL), lambda i: (seq(i), 0, col))
    return pl.pallas_call(
        _post_kernel,
        grid=(t // tm,),
        in_specs=[tok(0),
                  tok(1), tok(2), tok(3), tok(4),
                  tok(0),
                  modv(2), modv(3), modv(4),
                  pl.BlockSpec((1, D_MODEL), const2),
                  pl.BlockSpec((1, D_MODEL), const2),
                  pl.BlockSpec((1, D_MODEL), const2),
                  resident((N_SGU_GROUPS, BLOCK, BLOCK), const3),
                  resident((N_SGU_GROUPS, BLOCK, BLOCK), const3),
                  resident((D_MODEL, D_MODEL), const2),
                  resident((D_MODEL, D_MODEL), const2),
                  resident((D_MODEL, D_MODEL), const2),
                  resident((D_MODEL, N_EXPERTS), const2)],
        out_specs=[pl.BlockSpec((tm, D_MODEL), lambda i: (i, 0)),
                   pl.BlockSpec((tm, D_MODEL), lambda i: (i, 0)),
                   pl.BlockSpec((tm, N_EXPERTS), lambda i: (i, 0))],
        out_shape=[jax.ShapeDtypeStruct((t, D_MODEL), F32),
                   jax.ShapeDtypeStruct((t, D_MODEL), BF16),
                   jax.ShapeDtypeStruct((t, N_EXPERTS), F32)],
        scratch_shapes=[pltpu.VMEM((tm, D_MODEL), BF16)],
        compiler_params=_cparams(("arbitrary",)),
        name="post_mixer",
    )(attn_o, z, z, z, z, x2, mod3, mod3, mod3, norm2_g, ln_g, ln_b, ws, bs, w_oa, w_os, w_out, w_r)


def _ffn_kernel(x_ref, wg_ref, wu_ref, wd_ref, tw_ref, o_ref, acc_ref):
    f = pl.program_id(2)
    x = x_ref[...]
    g = jnp.dot(x, wg_ref[...], preferred_element_type=F32)
    up = jnp.dot(x, wu_ref[...], preferred_element_type=F32)
    hmid = (g * _sigmoid(g) * up).astype(BF16)
    part = jnp.dot(hmid, wd_ref[...], preferred_element_type=F32)

    @pl.when(f == 0)
    def _():
        acc_ref[...] = part

    @pl.when(f > 0)
    def _():
        acc_ref[...] += part

    @pl.when(f == pl.num_programs(2) - 1)
    def _():
        o_ref[...] = acc_ref[...] * tw_ref[...]


def _expert_ffn(xe, w_gate, w_up, w_down, top_w, tc, fc):
    e, c, d = xe.shape
    return pl.pallas_call(
        _ffn_kernel,
        grid=(e, c // tc, D_FF // fc),
        in_specs=[pl.BlockSpec((None, tc, d), lambda ei, ci, fi: (ei, ci, 0)),
                  pl.BlockSpec((None, d, fc), lambda ei, ci, fi: (ei, 0, fi)),
                  pl.BlockSpec((None, d, fc), lambda ei, ci, fi: (ei, 0, fi)),
                  pl.BlockSpec((None, fc, d), lambda ei, ci, fi: (ei, fi, 0)),
                  pl.BlockSpec((None, tc, 1), lambda ei, ci, fi: (ei, ci, 0))],
        out_specs=pl.BlockSpec((None, tc, d), lambda ei, ci, fi: (ei, ci, 0)),
        out_shape=jax.ShapeDtypeStruct((e, c, d), F32),
        scratch_shapes=[pltpu.VMEM((tc, d), F32)],
        compiler_params=_cparams(("arbitrary", "arbitrary", "arbitrary")),
        name="expert_ffn",
    )(xe, w_gate, w_up, w_down, top_w)


def _trunk(x, mod3, seq_base, prm):
    b, s, d = x.shape
    t = b * s
    x2 = x.reshape(t, d)
    mod_g = lax.slice_in_dim(mod3, seq_base, seq_base + b, axis=0)
    z = _in_projection(x2, mod_g, prm["norm1_g"], prm["w_in"], prm["q_g"], prm["k_g"], s, min(1024, s))
    attn_o = _attention(z, prm["sink"], s)
    x1, h2, aff = _post_mixer(attn_o, z, x2, mod_g, prm["norm2_g"], prm["ln_g"], prm["ln_b"], prm["ws"],
                              prm["bs"], prm["w_oa"], prm["w_os"], prm["w_out"], prm["w_r"], s, 256)
    cap = CAPACITY_FACTOR * t // N_EXPERTS
    top_w, top_idx = lax.top_k(aff.T, cap)
    xe = h2[top_idx]
    y = _expert_ffn(xe, prm["w_gate"], prm["w_up"], prm["w_down"], top_w[..., None], min(1024, cap), 256)
    moe = jnp.zeros((t, d), F32).at[top_idx.reshape(-1)].add(y.reshape(-1, d))
    g2 = jnp.repeat(mod_g[:, 0, 5 * d:6 * d], s, axis=0)
    return (x1 + g2 * moe).reshape(b, s, d)


def kernel(x_prompt, x_sample, c_prompt, c_sample, w_ada, b_ada, norm1_g, norm2_g, w_in, q_norm_g, k_norm_g,
           attn_sink, sgu_ln_g, sgu_ln_b, sgu_w, sgu_b, w_o_attn, w_o_sgu, w_out, w_router, w_gate, w_up,
           w_down):
    assert w_ada.shape[0] == 1
    bp, bs_ = x_prompt.shape[0], x_sample.shape[0]
    assert bp + bs_ <= N_SEQ_PAD
    d = D_MODEL
    c_pad = jnp.zeros((N_SEQ_PAD, d), F32).at[:bp].set(c_prompt).at[bp:bp + bs_].set(c_sample)
    mod = _modulation(c_pad, w_ada[0], b_ada[0])
    mod3 = mod.reshape(N_SEQ_PAD, 1, 6 * d)

    w = w_in[0]
    q_end = N_Q_HEADS * HEAD_DIM
    kv_end = q_end + 2 * KV_WIDTH
    w_in_r = jnp.concatenate([w[:, :q_end], w[:, kv_end:], w[:, q_end:kv_end]], axis=1).astype(BF16)
    prm = dict(
        norm1_g=norm1_g[0].reshape(1, d), norm2_g=norm2_g[0].reshape(1, d), w_in=w_in_r,
        q_g=q_norm_g[0].reshape(1, HEAD_DIM), k_g=k_norm_g[0].reshape(1, HEAD_DIM), sink=attn_sink[0],
        ln_g=sgu_ln_g[0].reshape(1, d), ln_b=sgu_ln_b[0].reshape(1, d),
        ws=sgu_w[0].astype(BF16),
        bs=jnp.broadcast_to(sgu_b[0][:, :, None], (N_SGU_GROUPS, BLOCK, BLOCK)),
        w_oa=w_o_attn[0].astype(BF16), w_os=w_o_sgu[0].astype(BF16), w_out=w_out[0].astype(BF16),
        w_r=w_router[0], w_gate=w_gate[0].astype(BF16), w_up=w_up[0].astype(BF16),
        w_down=w_down[0].astype(BF16))
    y_prompt = _trunk(x_prompt, mod3, 0, prm)
    y_sample = _trunk(x_sample, mod3, bp, prm)
    return (y_prompt, y_sample)
```

```python
import functools

import jax
import jax.numpy as jnp
from jax import lax
from jax.experimental import pallas as pl
from jax.experimental.pallas import tpu as pltpu

F32 = jnp.float32
BF16 = jnp.bfloat16
I32 = jnp.int32

D_MODEL = 2048
HEAD_DIM = 128
N_Q_HEADS = 16
N_KV_HEADS = 4
Q_PER_KV = N_Q_HEADS // N_KV_HEADS
KV_WIDTH = N_KV_HEADS * HEAD_DIM
BLOCK = 128
N_SGU_GROUPS = 16
N_EXPERTS = 16
CAPACITY_FACTOR = 2
D_FF = 2816
NORM_EPS = 1e-6
IN_WIDTH = 11264
N_SEQ_PAD = 8

LANES = 128
SUBLANES = 8
BF16_ROWS = 16
MXU_DIM = 256

COL_TILE = 1024
Q_TILES = (0, 2)
GELU_TILES = (2, 6)
SIG_TILES = (6, 10)
KV_TILE = 10
N_COL_TILES = IN_WIDTH // COL_TILE

COMBINE_TILE = 256
PIECE = 32

VMEM_LIMIT = 56 * 1024 * 1024


def _cparams(sem):
    return pltpu.CompilerParams(dimension_semantics=sem, vmem_limit_bytes=VMEM_LIMIT)


def _gelu_tanh(x):
    c = 0.7978845608028654
    return 0.5 * x * (1.0 + jnp.tanh(c * (x + 0.044715 * (x * x * x))))


def _sigmoid(x):
    return 1.0 / (1.0 + jnp.exp(-x))


def _ones_where(cond, dtype=F32):
    return jnp.where(cond, 1.0, 0.0).astype(dtype)


def _mod_kernel(c_ref, w_ref, b_ref, o_ref):
    c = c_ref[...]
    s = c * _sigmoid(c)
    o_ref[...] = jnp.dot(s, w_ref[...], precision=lax.Precision.HIGHEST,
                         preferred_element_type=F32) + b_ref[...]


def _modulation(c_pad, w_ada, b_ada):
    n = w_ada.shape[1]
    tn = 1024
    return pl.pallas_call(
        _mod_kernel,
        grid=(n // tn,),
        in_specs=[pl.BlockSpec((N_SEQ_PAD, D_MODEL), lambda j: (0, 0)),
                  pl.BlockSpec((D_MODEL, tn), lambda j: (0, j)),
                  pl.BlockSpec((1, tn), lambda j: (0, j))],
        out_specs=pl.BlockSpec((N_SEQ_PAD, tn), lambda j: (0, j)),
        out_shape=jax.ShapeDtypeStruct((N_SEQ_PAD, n), F32),
        compiler_params=_cparams(("arbitrary",)),
        name="adaln_mod",
    )(c_pad, w_ada, b_ada.reshape(1, n))


def _inproj_kernel(x_ref, sh_ref, sc_ref, g_ref, w_ref, qg_ref, kg_ref, o_ref, h_ref):
    j = pl.program_id(1)

    @pl.when(j == 0)
    def _():
        x = x_ref[...]
        ms = jnp.mean(x * x, axis=-1, keepdims=True)
        y = x * lax.rsqrt(ms + NORM_EPS) * g_ref[...]
        h_ref[...] = (y * (1.0 + sc_ref[...]) + sh_ref[...]).astype(BF16)

    def head_norm(sub, g):
        ms = jnp.mean(sub * sub, axis=-1, keepdims=True)
        return sub * lax.rsqrt(ms + NORM_EPS) * g

    acc = jnp.dot(h_ref[...], w_ref[...], preferred_element_type=F32)

    @pl.when(j < Q_TILES[1])
    def _():
        g = qg_ref[...] * (HEAD_DIM ** -0.5)
        for hh in range(COL_TILE // HEAD_DIM):
            cols = slice(hh * HEAD_DIM, (hh + 1) * HEAD_DIM)
            o_ref[:, cols] = head_norm(acc[:, cols], g).astype(BF16)

    @pl.when((j >= GELU_TILES[0]) & (j < GELU_TILES[1]))
    def _():
        o_ref[...] = _gelu_tanh(acc).astype(BF16)

    @pl.when((j >= SIG_TILES[0]) & (j < SIG_TILES[1]))
    def _():
        o_ref[...] = _sigmoid(acc).astype(BF16)

    @pl.when(j == KV_TILE)
    def _():
        g = kg_ref[...]
        for hh in range(N_KV_HEADS):
            cols = slice(hh * HEAD_DIM, (hh + 1) * HEAD_DIM)
            o_ref[:, cols] = head_norm(acc[:, cols], g).astype(BF16)
        o_ref[:, KV_WIDTH:] = acc[:, KV_WIDTH:].astype(BF16)


def _in_projection(x2, mod3, norm_g, w_in_r, q_g, k_g, seq_len, tm):
    t = x2.shape[0]
    seq = lambda i: (i * tm) // seq_len
    return pl.pallas_call(
        _inproj_kernel,
        grid=(t // tm, N_COL_TILES),
        in_specs=[pl.BlockSpec((tm, D_MODEL), lambda i, j: (i, 0)),
                  pl.BlockSpec((None, 1, D_MODEL), lambda i, j: (seq(i), 0, 0)),
                  pl.BlockSpec((None, 1, D_MODEL), lambda i, j: (seq(i), 0, 1)),
                  pl.BlockSpec((1, D_MODEL), lambda i, j: (0, 0)),
                  pl.BlockSpec((D_MODEL, COL_TILE), lambda i, j: (0, j)),
                  pl.BlockSpec((1, HEAD_DIM), lambda i, j: (0, 0)),
                  pl.BlockSpec((1, HEAD_DIM), lambda i, j: (0, 0))],
        out_specs=pl.BlockSpec((tm, COL_TILE), lambda i, j: (i, j)),
        out_shape=jax.ShapeDtypeStruct((t, IN_WIDTH), BF16),
        scratch_shapes=[pltpu.VMEM((tm, D_MODEL), BF16)],
        compiler_params=_cparams(("arbitrary", "arbitrary")),
        name="norm_inproj",
    )(x2, mod3, mod3, norm_g, w_in_r, q_g, k_g)


def _alibi_slope(head):
    return 2.0 ** (-8.0 * (head + 1) / N_Q_HEADS)


def _attn_kernel(sink_ref, q_ref, kp_ref, kc_ref, kn_ref, vp_ref, vc_ref, vn_ref, o_ref, *, blocks_per_seq):
    n = pl.program_id(0)
    pos = n % blocks_per_seq
    first = pos == 0
    last = pos == blocks_per_seq - 1

    qi = lax.broadcasted_iota(I32, (BLOCK, 3 * BLOCK), 0)
    kj = lax.broadcasted_iota(I32, (BLOCK, 3 * BLOCK), 1)
    rel = kj - BLOCK - qi
    absrel = jnp.abs(rel)
    valid = absrel <= BLOCK
    valid = valid & jnp.logical_not(first & (kj < BLOCK)) & jnp.logical_not(last & (kj >= 2 * BLOCK))
    absrel_f = absrel.astype(F32)
    neg_inf = jnp.float32(-jnp.inf)

    for h in range(N_KV_HEADS):
        kv_cols = slice(h * HEAD_DIM, (h + 1) * HEAD_DIM)
        kcat = jnp.concatenate([kp_ref[:, kv_cols], kc_ref[:, kv_cols], kn_ref[:, kv_cols]], axis=0)
        vcat = jnp.concatenate([vp_ref[:, kv_cols], vc_ref[:, kv_cols], vn_ref[:, kv_cols]], axis=0)
        for g in range(Q_PER_KV):
            head = h * Q_PER_KV + g
            cols = slice(head * HEAD_DIM, (head + 1) * HEAD_DIM)
            s = lax.dot_general(q_ref[:, cols], kcat, (((1,), (1,)), ((), ())),
                                preferred_element_type=F32)
            s = jnp.where(valid, s - _alibi_slope(head) * absrel_f, neg_inf)
            sink = sink_ref[head]
            m = jnp.maximum(jnp.max(s, axis=-1, keepdims=True), sink)
            p = jnp.exp(s - m)
            denom = jnp.sum(p, axis=-1, keepdims=True) + jnp.exp(sink - m)
            o = jnp.dot(p.astype(BF16), vcat, preferred_element_type=F32)
            o_ref[:, cols] = (o / denom).astype(BF16)


def _attention(z, sink, seq_len):
    t = z.shape[0]
    nb = t // BLOCK
    bps = seq_len // BLOCK
    k_col = (IN_WIDTH - 2 * KV_WIDTH) // KV_WIDTH
    v_col = k_col + 1
    prev = lambda n, s: (jnp.maximum(n - 1, 0), k_col)
    cur = lambda n, s: (n, k_col)
    nxt = lambda n, s: (jnp.minimum(n + 1, nb - 1), k_col)
    vprev = lambda n, s: (jnp.maximum(n - 1, 0), v_col)
    vcur = lambda n, s: (n, v_col)
    vnxt = lambda n, s: (jnp.minimum(n + 1, nb - 1), v_col)
    kv_spec = lambda f: pl.BlockSpec((BLOCK, KV_WIDTH), f)
    grid_spec = pltpu.PrefetchScalarGridSpec(
        num_scalar_prefetch=1,
        grid=(nb,),
        in_specs=[pl.BlockSpec((BLOCK, D_MODEL), lambda n, s: (n, 0)),
                  kv_spec(prev), kv_spec(cur), kv_spec(nxt),
                  kv_spec(vprev), kv_spec(vcur), kv_spec(vnxt)],
        out_specs=pl.BlockSpec((BLOCK, D_MODEL), lambda n, s: (n, 0)),
    )
    return pl.pallas_call(
        functools.partial(_attn_kernel, blocks_per_seq=bps),
        grid_spec=grid_spec,
        out_shape=jax.ShapeDtypeStruct((t, D_MODEL), BF16),
        compiler_params=_cparams(("arbitrary",)),
        name="window_attn",
    )(sink, z, z, z, z, z, z, z)


def _post_kernel(attn_ref, u_ref, vg_ref, ga_ref, gs_ref, x_ref, g1_ref, sh2_ref, sc2_ref, n2g_ref,
                 lng_ref, lnb_ref, ws_ref, bs_ref, woa_ref, wos_ref, wout_ref, wr_ref,
                 x1_ref, h2p_ref, aff_ref, afft_ref, sgu_ref):
    tm = x_ref.shape[0]
    vg = vg_ref[...].astype(F32)
    mu = jnp.mean(vg, axis=-1, keepdims=True)
    cen = vg - mu
    var = jnp.mean(cen * cen, axis=-1, keepdims=True)
    vn = (cen * lax.rsqrt(var + NORM_EPS) * lng_ref[...] + lnb_ref[...]).astype(BF16)
    for c in range(tm // BLOCK):
        rows = slice(c * BLOCK, (c + 1) * BLOCK)
        for g in range(N_SGU_GROUPS):
            cols = slice(g * BLOCK, (g + 1) * BLOCK)
            mixed = jnp.dot(ws_ref[g], vn[rows, cols], preferred_element_type=F32) + bs_ref[g]
            sgu_ref[rows, cols] = (u_ref[rows, cols].astype(F32) * mixed).astype(BF16)
    a = jnp.dot(attn_ref[...], woa_ref[...], preferred_element_type=F32)
    s = jnp.dot(sgu_ref[...], wos_ref[...], preferred_element_type=F32)
    merged = (ga_ref[...].astype(F32) * a + gs_ref[...].astype(F32) * s).astype(BF16)
    mix = jnp.dot(merged, wout_ref[...], preferred_element_type=F32)
    x1 = x_ref[...] + g1_ref[...] * mix
    x1_ref[...] = x1
    ms = jnp.mean(x1 * x1, axis=-1, keepdims=True)
    h2 = x1 * lax.rsqrt(ms + NORM_EPS) * n2g_ref[...]
    h2 = h2 * (1.0 + sc2_ref[...]) + sh2_ref[...]
    h2b = h2.astype(BF16)
    half = D_MODEL // 2
    n_slab = half // LANES
    lo = lax.shift_right_logical(lax.bitcast_convert_type(h2b[:, :half].astype(F32), I32), 16)
    hi = lax.bitcast_convert_type(h2b[:, half:].astype(F32), I32) & jnp.int32(-65536)
    word = hi | lo
    for j in range(n_slab):
        h2p_ref[pl.ds(j, tm, stride=n_slab), :] = word[:, j * LANES:(j + 1) * LANES]
    h_lo = (h2 - h2b.astype(F32)).astype(BF16)
    r1 = jnp.dot(h2b, wr_ref[...], preferred_element_type=F32)
    r2 = jnp.dot(h_lo, wr_ref[:, :LANES], preferred_element_type=F32)
    logits = r1[:, :LANES] + r1[:, LANES:] + r2
    lane = lax.broadcasted_iota(I32, logits.shape, 1)
    logits = jnp.where(lane < N_EXPERTS, logits, -jnp.inf)
    logits = logits - jnp.max(logits, axis=-1, keepdims=True)
    ex = jnp.exp(logits)
    aff = ex / jnp.sum(ex, axis=-1, keepdims=True)
    aff_ref[...] = aff[:, :N_EXPERTS]
    afft_ref[...] = aff.T[:N_EXPERTS, :]


def _post_mixer(attn_o, z, x2, mod3, norm2_g, ln_g, ln_b, ws, bs, w_oa, w_os, w_out, w_r2, seq_len, tm):
    t = x2.shape[0]
    n_slab = D_MODEL // 2 // LANES
    seq = lambda i: (i * tm) // seq_len
    const2 = lambda i: (0, 0)
    const3 = lambda i: (0, 0, 0)
    resident = lambda shape, imap: pl.BlockSpec(shape, imap, pipeline_mode=pl.Buffered(1))
    tok = lambda col: pl.BlockSpec((tm, D_MODEL), lambda i: (i, col))
    modv = lambda col: pl.BlockSpec((None, 1, D_MODEL), lambda i: (seq(i), 0, col))
    return pl.pallas_call(
        _post_kernel,
        grid=(t // tm,),
        in_specs=[tok(0),
                  tok(1), tok(2), tok(3), tok(4),
                  tok(0),
                  modv(2), modv(3), modv(4),
                  pl.BlockSpec((1, D_MODEL), const2),
                  pl.BlockSpec((1, D_MODEL), const2),
                  pl.BlockSpec((1, D_MODEL), const2),
                  resident((N_SGU_GROUPS, BLOCK, BLOCK), const3),
                  resident((N_SGU_GROUPS, BLOCK, BLOCK), const3),
                  resident((D_MODEL, D_MODEL), const2),
                  resident((D_MODEL, D_MODEL), const2),
                  resident((D_MODEL, D_MODEL), const2),
                  resident((D_MODEL, 2 * LANES), const2)],
        out_specs=[pl.BlockSpec((tm, D_MODEL), lambda i: (i, 0)),
                   pl.BlockSpec((tm * n_slab, LANES), lambda i: (i, 0)),
                   pl.BlockSpec((tm, N_EXPERTS), lambda i: (i, 0)),
                   pl.BlockSpec((N_EXPERTS, tm), lambda i: (0, i))],
        out_shape=[jax.ShapeDtypeStruct((t, D_MODEL), F32),
                   jax.ShapeDtypeStruct((t * n_slab, LANES), I32),
                   jax.ShapeDtypeStruct((t, N_EXPERTS), F32),
                   jax.ShapeDtypeStruct((N_EXPERTS, t), F32)],
        scratch_shapes=[pltpu.VMEM((tm, D_MODEL), BF16)],
        compiler_params=_cparams(("arbitrary",)),
        name="post_mixer",
    )(attn_o, z, z, z, z, x2, mod3, mod3, mod3, norm2_g, ln_g, ln_b, ws, bs, w_oa, w_os, w_out, w_r2)


def _route_kernel(aff_ref, idx_ref, tau_ref, need_ref, pref_ref, eqpref_ref, bits_ref, taus_ref, *, cap):
    n_exp, n_rows, _ = aff_ref.shape
    n_tok = n_rows * LANES
    bits_ref[...] = lax.bitcast_convert_type(aff_ref[...], I32)

    def bisect(i, v):
        cand = v | lax.shift_left(jnp.int32(1), 30 - i)
        ge = _ones_where(bits_ref[...] >= cand)
        cnt = jnp.sum(jnp.sum(ge, axis=1, keepdims=True), axis=2, keepdims=True)
        return jnp.where(cnt >= cap, cand, v)

    taus_ref[...] = lax.fori_loop(0, 31, bisect, jnp.zeros((n_exp, 1, LANES), I32))

    li = lax.broadcasted_iota(I32, (LANES, LANES), 0)
    lj = lax.broadcasted_iota(I32, (LANES, LANES), 1)
    upper_incl = _ones_where(li <= lj, BF16)
    ones_sq = jnp.ones((LANES, LANES), BF16)
    ri = lax.broadcasted_iota(I32, (n_rows, n_rows), 0)
    rj = lax.broadcasted_iota(I32, (n_rows, n_rows), 1)
    lower_strict = _ones_where(rj < ri, BF16)
    upper_strict = _ones_where(ri < rj, BF16)
    ones_rows = jnp.ones((SUBLANES, LANES), BF16)
    nt_dims = (((1,), (1,)), ((), ()))

    def incl_cumsum(mb):
        local = jnp.dot(mb, upper_incl, preferred_element_type=F32)
        totb = jnp.dot(mb, ones_sq, preferred_element_type=F32)
        prefc = jnp.dot(lower_strict, totb.astype(BF16), preferred_element_type=F32)
        return local + prefc

    def row_prefix(mb):
        tot_row = lax.dot_general(ones_rows, mb, nt_dims, preferred_element_type=F32)
        pref_row = jnp.dot(tot_row.astype(BF16), upper_strict, preferred_element_type=F32)
        return tot_row, pref_row

    def per_expert(e, carry):
        t = taus_ref[e]
        b = bits_ref[e]
        gt = b > t
        eq = b == t
        eqb = _ones_where(eq, BF16)
        need = cap - jnp.sum(_ones_where(gt))
        sel = gt | (eq & (incl_cumsum(eqb) <= need))
        m = _ones_where(sel, BF16)
        glob = incl_cumsum(m)
        tot_row, pref_row = row_prefix(m)
        incl_row = pref_row + tot_row
        _, eq_pref_row = row_prefix(eqb)
        pref_ref[e] = pref_row.astype(I32)
        eqpref_ref[e] = eq_pref_row
        tau_ref[e] = jnp.broadcast_to(lax.bitcast_convert_type(t, F32), (SUBLANES, LANES))
        need_ref[e] = jnp.full((SUBLANES, LANES), need, F32)
        ghi = jnp.floor(glob * (1.0 / MXU_DIM))
        glo = (glob - MXU_DIM * ghi).astype(BF16)
        ghi = ghi.astype(BF16)
        pr = pref_row[0:1, :]
        ir = incl_row[0:1, :]

        def chunk(c, carry2):
            s_r = (c * LANES + lax.broadcasted_iota(I32, (LANES, n_rows), 0)).astype(F32)
            onehot = _ones_where((pr <= s_r) & (s_r < ir), BF16)
            rowid = jnp.sum(_ones_where(ir <= s_r), axis=-1, keepdims=True)
            grow = (MXU_DIM * jnp.dot(onehot, ghi, preferred_element_type=F32)
                    + jnp.dot(onehot, glo, preferred_element_type=F32))
            s_l = (c * LANES + lax.broadcasted_iota(I32, (LANES, LANES), 0)).astype(F32)
            inrow = jnp.sum(_ones_where(grow <= s_l), axis=-1, keepdims=True)
            tok = jnp.minimum(rowid * LANES + inrow, n_tok - 1.0)
            tok_t = jnp.broadcast_to(tok, (LANES, LANES)).T
            idx_ref[e, pl.ds(c, 1), :] = tok_t[0:1, :].astype(I32)
            return carry2

        lax.fori_loop(0, cap // LANES, chunk, 0)
        return carry

    lax.fori_loop(0, n_exp, per_expert, 0)


def _route(aff3, cap):
    n_exp, n_rows, _ = aff3.shape
    rep = lambda dt, w: jax.ShapeDtypeStruct((n_exp, SUBLANES, w), dt)
    return pl.pallas_call(
        functools.partial(_route_kernel, cap=cap),
        out_shape=[jax.ShapeDtypeStruct((n_exp, cap // LANES, LANES), I32),
                   rep(F32, LANES), rep(F32, LANES), rep(I32, n_rows), rep(F32, n_rows)],
        scratch_shapes=[pltpu.VMEM((n_exp, n_rows, LANES), I32), pltpu.VMEM((n_exp, 1, LANES), I32)],
        compiler_params=pltpu.CompilerParams(vmem_limit_bytes=VMEM_LIMIT),
        name="ec_route",
    )(aff3)


def _ffn_kernel(idx_ref, h2p_hbm, wg_ref, wu_ref, wd_ref, o_ref, xraw_ref, xb_ref, acc_ref, sem, *, tc):
    ct_n = pl.num_programs(1)
    f = pl.program_id(2)
    tile = pl.program_id(0) * ct_n + pl.program_id(1)
    n_tiles = pl.num_programs(0) * ct_n
    n_slab = xraw_ref.shape[1] // tc

    def gather(tile_id, slot):
        def body(s, carry):
            tok = idx_ref[tile_id * tc + s]
            pltpu.make_async_copy(h2p_hbm.at[pl.ds(pl.multiple_of(tok * n_slab, n_slab), n_slab), :],
                                  xraw_ref.at[slot, pl.ds(pl.multiple_of(s * n_slab, n_slab), n_slab), :],
                                  sem.at[slot]).start()
            return carry
        lax.fori_loop(0, tc, body, 0)

    @pl.when(f == 0)
    def _():
        slot = tile % 2

        @pl.when(tile == 0)
        def _():
            gather(tile, slot)

        pltpu.make_async_copy(xraw_ref.at[slot], xraw_ref.at[slot], sem.at[slot]).wait()
        half = n_slab * LANES
        for j in range(n_slab):
            w = xraw_ref[slot, pl.ds(j, tc, stride=n_slab), :]
            lo = lax.bitcast_convert_type(lax.shift_left(w, 16), F32)
            hi = lax.bitcast_convert_type(w & jnp.int32(-65536), F32)
            xb_ref[:, j * LANES:(j + 1) * LANES] = lo.astype(BF16)
            xb_ref[:, half + j * LANES:half + (j + 1) * LANES] = hi.astype(BF16)

        @pl.when(tile + 1 < n_tiles)
        def _():
            gather(tile + 1, 1 - slot)

    x = xb_ref[...]
    g = jnp.dot(x, wg_ref[...], preferred_element_type=F32)
    up = jnp.dot(x, wu_ref[...], preferred_element_type=F32)
    hmid = (g * _sigmoid(g) * up).astype(BF16)
    part = jnp.dot(hmid, wd_ref[...], preferred_element_type=F32)

    @pl.when(f == 0)
    def _():
        acc_ref[...] = part

    @pl.when(f > 0)
    def _():
        acc_ref[...] += part

    @pl.when(f == pl.num_programs(2) - 1)
    def _():
        o_ref[...] = acc_ref[...].astype(BF16)


def _expert_ffn(idx_flat, h2p, w_gate, w_up, w_down, cap, tc, fc):
    n_exp, d, d_ff = w_gate.shape
    n_slab = d // 2 // LANES
    nct = cap // tc
    grid_spec = pltpu.PrefetchScalarGridSpec(
        num_scalar_prefetch=1,
        grid=(n_exp, nct, d_ff // fc),
        in_specs=[pl.BlockSpec(memory_space=pl.ANY),
                  pl.BlockSpec((None, d, fc), lambda ei, ci, fi, idx: (ei, 0, fi)),
                  pl.BlockSpec((None, d, fc), lambda ei, ci, fi, idx: (ei, 0, fi)),
                  pl.BlockSpec((None, fc, d), lambda ei, ci, fi, idx: (ei, fi, 0))],
        out_specs=pl.BlockSpec((tc, d), lambda ei, ci, fi, idx: (ei * nct + ci, 0)),
        scratch_shapes=[pltpu.VMEM((2, tc * n_slab, LANES), I32),
                        pltpu.VMEM((tc, d), BF16),
                        pltpu.VMEM((tc, d), F32),
                        pltpu.SemaphoreType.DMA((2,))],
    )
    return pl.pallas_call(
        functools.partial(_ffn_kernel, tc=tc),
        grid_spec=grid_spec,
        out_shape=jax.ShapeDtypeStruct((n_exp * cap, d), BF16),
        compiler_params=_cparams(("arbitrary", "arbitrary", "arbitrary")),
        name="expert_ffn",
    )(idx_flat, h2p, w_gate, w_up, w_down)


def _combine_kernel(tab_ref, aff_ref, tau_ref, need_ref, eqs_ref, x1_ref, g2_ref, y_hbm, o_ref,
                    ybuf_ref, acc_ref, sem, *, cap, n_tile):
    i = pl.program_id(0)
    tm, n_exp = aff_ref.shape
    total_rows = n_exp * cap
    lane_e = lax.broadcasted_iota(I32, (1, n_exp), 1)

    def piece_copy(src_row, dst_piece):
        return pltpu.make_async_copy(
            y_hbm.at[pl.ds(pl.multiple_of(src_row, BF16_ROWS), PIECE), :],
            ybuf_ref.at[pl.ds(pl.multiple_of(dst_piece * PIECE, PIECE), PIECE), :], sem)

    base_vec = jnp.zeros((1, n_exp), F32)
    n_piece = jnp.int32(0)
    for e in range(n_exp):
        n0 = tab_ref[e * (n_tile + 1) + i]
        n1 = tab_ref[e * (n_tile + 1) + i + 1]
        first = e * cap + n0
        aligned = (first // BF16_ROWS) * BF16_ROWS
        span = first - aligned + (n1 - n0)
        pieces = jnp.where(n1 > n0, (span + PIECE - 1) // PIECE, 0)
        start = jnp.minimum(aligned, total_rows - pieces * PIECE)
        base_vec = jnp.where(lane_e == e, (n_piece * PIECE + first - start).astype(F32), base_vec)

        def issue(p, carry, start=start, n_piece=n_piece):
            piece_copy(start + p * PIECE, n_piece + p).start()
            return carry

        lax.fori_loop(0, pieces, issue, 0)
        n_piece = n_piece + pieces

    a = aff_ref[...]
    tau = tau_ref[...]
    eq = a == tau
    ti = lax.broadcasted_iota(I32, (tm, tm), 0)
    tj = lax.broadcasted_iota(I32, (tm, tm), 1)
    eq_rank = eqs_ref[...] + jnp.dot(_ones_where(tj <= ti, BF16), _ones_where(eq, BF16),
                                     preferred_element_type=F32)
    sel = (a > tau) | (eq & (eq_rank <= need_ref[...]))
    wm = jnp.where(sel, a, 0.0)
    rank = jnp.dot(_ones_where(tj < ti, BF16), _ones_where(sel, BF16), preferred_element_type=F32)
    bufrow = base_vec + rank

    def wait_piece(p, carry):
        piece_copy(p * 0, p).wait()
        return carry

    lax.fori_loop(0, n_piece, wait_piece, 0)

    pieces_per_chunk = MXU_DIM // PIECE
    n_chunk = (n_piece + pieces_per_chunk - 1) // pieces_per_chunk

    def zero_tail(p, carry):
        ybuf_ref[pl.ds(pl.multiple_of(p * PIECE, PIECE), PIECE), :] = jnp.zeros((PIECE, ybuf_ref.shape[1]), BF16)
        return carry

    lax.fori_loop(n_piece, n_chunk * pieces_per_chunk, zero_tail, 0)

    acc_ref[...] = jnp.zeros_like(acc_ref)
    k_iota = lax.broadcasted_iota(I32, (tm, MXU_DIM), 1).astype(F32)

    def chunk(kc, carry):
        kbase = (kc * MXU_DIM).astype(F32)
        place = jnp.zeros((tm, MXU_DIM), F32)
        for e in range(n_exp):
            place = place + jnp.where(bufrow[:, e:e + 1] - kbase == k_iota, wm[:, e:e + 1], 0.0)
        rows = ybuf_ref[pl.ds(pl.multiple_of(kc * MXU_DIM, MXU_DIM), MXU_DIM), :]
        acc_ref[...] += jnp.dot(place.astype(BF16), rows, preferred_element_type=F32)
        return carry

    lax.fori_loop(0, n_chunk, chunk, 0)
    o_ref[...] = x1_ref[...] + g2_ref[...] * acc_ref[...]


def _combine(tab, aff, tau, need, eqs, x1, mod3, y, cap, seq_len, tm):
    t, d = x1.shape
    n_exp = aff.shape[1]
    n_tile = t // tm
    seq = lambda i, tab_: ((i * tm) // seq_len, 0, 5)
    max_pieces = n_exp * (tm // PIECE + 2)
    max_rows = -(-max_pieces * PIECE // MXU_DIM) * MXU_DIM
    grid_spec = pltpu.PrefetchScalarGridSpec(
        num_scalar_prefetch=1,
        grid=(n_tile,),
        in_specs=[pl.BlockSpec((tm, n_exp), lambda i, tab_: (i, 0)),
                  pl.BlockSpec((1, n_exp), lambda i, tab_: (0, 0)),
                  pl.BlockSpec((1, n_exp), lambda i, tab_: (0, 0)),
                  pl.BlockSpec((None, 1, n_exp), lambda i, tab_: (i, 0, 0)),
                  pl.BlockSpec((tm, d), lambda i, tab_: (i, 0)),
                  pl.BlockSpec((None, 1, d), seq),
                  pl.BlockSpec(memory_space=pl.ANY)],
        out_specs=pl.BlockSpec((tm, d), lambda i, tab_: (i, 0)),
        scratch_shapes=[pltpu.VMEM((max_rows, d), BF16),
                        pltpu.VMEM((tm, d), F32),
                        pltpu.SemaphoreType.DMA(())],
    )
    return pl.pallas_call(
        functools.partial(_combine_kernel, cap=cap, n_tile=n_tile),
        grid_spec=grid_spec,
        out_shape=jax.ShapeDtypeStruct((t, d), F32),
        compiler_params=_cparams(("arbitrary",)),
        name="ec_combine",
    )(tab, aff, tau, need, eqs, x1, mod3, y)


def _moe(h2p, aff, afft, x1, mod_g, w_gate, w_up, w_down, seq_len):
    t = aff.shape[0]
    n_exp = aff.shape[1]
    cap = CAPACITY_FACTOR * t // n_exp
    tm = COMBINE_TILE
    idx, tau, need, pref, eqpref = _route(afft.reshape(n_exp, t // LANES, LANES), cap)
    y = _expert_ffn(idx.reshape(-1), h2p, w_gate, w_up, w_down, cap, min(1024, cap), 256)
    rows_per_tile = tm // LANES
    tab = jnp.concatenate([pref[:, 0, ::rows_per_tile], jnp.full((n_exp, 1), cap, I32)], axis=1).reshape(-1)
    eqs = eqpref[:, 0, ::rows_per_tile].T.reshape(t // tm, 1, n_exp)
    return _combine(tab, aff, tau[:, 0, 0].reshape(1, n_exp), need[:, 0, 0].reshape(1, n_exp), eqs,
                    x1, mod_g, y, cap, seq_len, tm)


def _trunk(x, mod3, seq_base, prm):
    b, s, d = x.shape
    t = b * s
    x2 = x.reshape(t, d)
    mod_g = lax.slice_in_dim(mod3, seq_base, seq_base + b, axis=0)
    z = _in_projection(x2, mod_g, prm["norm1_g"], prm["w_in"], prm["q_g"], prm["k_g"], s, min(1024, s))
    attn_o = _attention(z, prm["sink"], s)
    x1, h2p, aff, afft = _post_mixer(attn_o, z, x2, mod_g, prm["norm2_g"], prm["ln_g"], prm["ln_b"], prm["ws"],
                                     prm["bs"], prm["w_oa"], prm["w_os"], prm["w_out"], prm["w_r2"], s, 256)
    out = _moe(h2p, aff, afft, x1, mod_g, prm["w_gate"], prm["w_up"], prm["w_down"], s)
    return out.reshape(b, s, d)


def kernel(x_prompt, x_sample, c_prompt, c_sample, w_ada, b_ada, norm1_g, norm2_g, w_in, q_norm_g, k_norm_g,
           attn_sink, sgu_ln_g, sgu_ln_b, sgu_w, sgu_b, w_o_attn, w_o_sgu, w_out, w_router, w_gate, w_up,
           w_down):
    assert w_ada.shape[0] == 1
    bp, bs_ = x_prompt.shape[0], x_sample.shape[0]
    assert bp + bs_ <= N_SEQ_PAD
    d = D_MODEL
    c_pad = jnp.zeros((N_SEQ_PAD, d), F32).at[:bp].set(c_prompt).at[bp:bp + bs_].set(c_sample)
    mod = _modulation(c_pad, w_ada[0], b_ada[0])
    mod3 = mod.reshape(N_SEQ_PAD, 1, 6 * d)

    w = w_in[0]
    q_end = N_Q_HEADS * HEAD_DIM
    kv_end = q_end + 2 * KV_WIDTH
    w_in_r = jnp.concatenate([w[:, :q_end], w[:, kv_end:], w[:, q_end:kv_end]], axis=1).astype(BF16)
    w_r = w_router[0]
    w_r_hi = w_r.astype(BF16)
    w_r_lo = (w_r - w_r_hi.astype(F32)).astype(BF16)
    pad = ((0, 0), (0, LANES - N_EXPERTS))
    w_r2 = jnp.concatenate([jnp.pad(w_r_hi, pad), jnp.pad(w_r_lo, pad)], axis=1)
    prm = dict(
        norm1_g=norm1_g[0].reshape(1, d), norm2_g=norm2_g[0].reshape(1, d), w_in=w_in_r,
        q_g=q_norm_g[0].reshape(1, HEAD_DIM), k_g=k_norm_g[0].reshape(1, HEAD_DIM), sink=attn_sink[0],
        ln_g=sgu_ln_g[0].reshape(1, d), ln_b=sgu_ln_b[0].reshape(1, d),
        ws=sgu_w[0].astype(BF16),
        bs=jnp.broadcast_to(sgu_b[0][:, :, None], (N_SGU_GROUPS, BLOCK, BLOCK)),
        w_oa=w_o_attn[0].astype(BF16), w_os=w_o_sgu[0].astype(BF16), w_out=w_out[0].astype(BF16),
        w_r2=w_r2, w_gate=w_gate[0].astype(BF16), w_up=w_up[0].astype(BF16),
        w_down=w_down[0].astype(BF16))
    y_prompt = _trunk(x_prompt, mod3, 0, prm)
    y_sample = _trunk(x_sample, mod3, bp, prm)
    return (y_prompt, y_sample)
```

```python
import functools

import jax
import jax.numpy as jnp
from jax import lax
from jax.experimental import pallas as pl
from jax.experimental.pallas import tpu as pltpu

F32 = jnp.float32
BF16 = jnp.bfloat16
I32 = jnp.int32

D_MODEL = 2048
HEAD_DIM = 128
N_Q_HEADS = 16
N_KV_HEADS = 4
Q_PER_KV = N_Q_HEADS // N_KV_HEADS
KV_WIDTH = N_KV_HEADS * HEAD_DIM
BLOCK = 128
N_SGU_GROUPS = 16
N_EXPERTS = 16
CAPACITY_FACTOR = 2
D_FF = 2816
NORM_EPS = 1e-6
IN_WIDTH = 11264
N_SEQ_PAD = 8

LANES = 128
SUBLANES = 8
BF16_ROWS = 16
MXU_DIM = 256

COL_TILE = 1024
Q_TILES = (0, 2)
GELU_TILES = (2, 6)
SIG_TILES = (6, 10)
KV_TILE = 10
N_COL_TILES = IN_WIDTH // COL_TILE

INPROJ_ROW_CHUNK = 256
ATTN_Q_BLOCKS = 2
COMBINE_TILE = 256
COMBINE_SLOT = 64

VMEM_LIMIT = 56 * 1024 * 1024


def _cparams(sem):
    return pltpu.CompilerParams(dimension_semantics=sem, vmem_limit_bytes=VMEM_LIMIT)


def _gelu_tanh(x):
    c = 0.7978845608028654
    return 0.5 * x * (1.0 + jnp.tanh(c * (x + 0.044715 * (x * x * x))))


def _sigmoid(x):
    return 1.0 / (1.0 + jnp.exp(-x))


def _ones_where(cond, dtype=F32):
    return jnp.where(cond, 1.0, 0.0).astype(dtype)


def _mod_kernel(c_ref, w_ref, b_ref, o_ref):
    c = c_ref[...]
    s = c * _sigmoid(c)
    o_ref[...] = jnp.dot(s, w_ref[...], precision=lax.Precision.HIGHEST,
                         preferred_element_type=F32) + b_ref[...]


def _modulation(c_pad, w_ada, b_ada):
    n = w_ada.shape[1]
    tn = 1024
    return pl.pallas_call(
        _mod_kernel,
        grid=(n // tn,),
        in_specs=[pl.BlockSpec((N_SEQ_PAD, D_MODEL), lambda j: (0, 0)),
                  pl.BlockSpec((D_MODEL, tn), lambda j: (0, j)),
                  pl.BlockSpec((1, tn), lambda j: (0, j))],
        out_specs=pl.BlockSpec((N_SEQ_PAD, tn), lambda j: (0, j)),
        out_shape=jax.ShapeDtypeStruct((N_SEQ_PAD, n), F32),
        compiler_params=_cparams(("arbitrary",)),
        name="adaln_mod",
    )(c_pad, w_ada, b_ada.reshape(1, n))


def _inproj_kernel(x_ref, sh_ref, sc_ref, g_ref, w_ref, qg_ref, kg_ref, o_ref, h_ref):
    j = pl.program_id(1)

    @pl.when(j == 0)
    def _():
        x = x_ref[...]
        ms = jnp.mean(x * x, axis=-1, keepdims=True)
        y = x * lax.rsqrt(ms + NORM_EPS) * g_ref[...]
        h_ref[...] = (y * (1.0 + sc_ref[...]) + sh_ref[...]).astype(BF16)

    def head_norm(sub, g):
        ms = jnp.mean(sub * sub, axis=-1, keepdims=True)
        return sub * lax.rsqrt(ms + NORM_EPS) * g

    def chunked(epilogue):
        for r in range(x_ref.shape[0] // INPROJ_ROW_CHUNK):
            rows = pl.ds(r * INPROJ_ROW_CHUNK, INPROJ_ROW_CHUNK)
            epilogue(rows, jnp.dot(h_ref[rows, :], w_ref[...], preferred_element_type=F32))

    @pl.when(j < Q_TILES[1])
    def _():
        g = qg_ref[...] * (HEAD_DIM ** -0.5)

        def epilogue(rows, acc):
            for hh in range(COL_TILE // HEAD_DIM):
                cols = slice(hh * HEAD_DIM, (hh + 1) * HEAD_DIM)
                o_ref[rows, cols] = head_norm(acc[:, cols], g).astype(BF16)
        chunked(epilogue)

    @pl.when((j >= GELU_TILES[0]) & (j < GELU_TILES[1]))
    def _():
        def epilogue(rows, acc):
            o_ref[rows, :] = _gelu_tanh(acc).astype(BF16)
        chunked(epilogue)

    @pl.when((j >= SIG_TILES[0]) & (j < SIG_TILES[1]))
    def _():
        def epilogue(rows, acc):
            o_ref[rows, :] = _sigmoid(acc).astype(BF16)
        chunked(epilogue)

    @pl.when(j == KV_TILE)
    def _():
        g = kg_ref[...]

        def epilogue(rows, acc):
            for hh in range(N_KV_HEADS):
                cols = slice(hh * HEAD_DIM, (hh + 1) * HEAD_DIM)
                o_ref[rows, cols] = head_norm(acc[:, cols], g).astype(BF16)
            o_ref[rows, KV_WIDTH:] = acc[:, KV_WIDTH:].astype(BF16)
        chunked(epilogue)


def _in_projection(x2, mod3, norm_g, w_in_r, q_g, k_g, seq_len, tm):
    t = x2.shape[0]
    seq = lambda i: (i * tm) // seq_len
    return pl.pallas_call(
        _inproj_kernel,
        grid=(t // tm, N_COL_TILES),
        in_specs=[pl.BlockSpec((tm, D_MODEL), lambda i, j: (i, 0)),
                  pl.BlockSpec((None, 1, D_MODEL), lambda i, j: (seq(i), 0, 0)),
                  pl.BlockSpec((None, 1, D_MODEL), lambda i, j: (seq(i), 0, 1)),
                  pl.BlockSpec((1, D_MODEL), lambda i, j: (0, 0)),
                  pl.BlockSpec((D_MODEL, COL_TILE), lambda i, j: (0, j)),
                  pl.BlockSpec((1, HEAD_DIM), lambda i, j: (0, 0)),
                  pl.BlockSpec((1, HEAD_DIM), lambda i, j: (0, 0))],
        out_specs=pl.BlockSpec((tm, COL_TILE), lambda i, j: (i, j)),
        out_shape=jax.ShapeDtypeStruct((t, IN_WIDTH), BF16),
        scratch_shapes=[pltpu.VMEM((tm, D_MODEL), BF16)],
        compiler_params=_cparams(("arbitrary", "arbitrary")),
        name="norm_inproj",
    )(x2, mod3, mod3, norm_g, w_in_r, q_g, k_g)


def _alibi_slope(head):
    return 2.0 ** (-8.0 * (head + 1) / N_Q_HEADS)


def _attn_kernel(sink_ref, q_ref, kp_ref, kc_ref, kn_ref, vp_ref, vc_ref, vn_ref, o_ref, *, steps_per_seq):
    n = pl.program_id(0)
    pos = n % steps_per_seq
    first = pos == 0
    last = pos == steps_per_seq - 1

    qi = lax.broadcasted_iota(I32, (BLOCK, 3 * BLOCK), 0)
    kj = lax.broadcasted_iota(I32, (BLOCK, 3 * BLOCK), 1)
    absrel = jnp.abs(kj - BLOCK - qi)
    in_window = absrel <= BLOCK
    absrel_f = absrel.astype(F32)
    pos_inf = jnp.float32(jnp.inf)

    for b in range(ATTN_Q_BLOCKS):
        q_rows = slice(b * BLOCK, (b + 1) * BLOCK)
        valid = in_window
        if b == 0:
            valid = valid & jnp.logical_not(first & (kj < BLOCK))
        if b == ATTN_Q_BLOCKS - 1:
            valid = valid & jnp.logical_not(last & (kj >= 2 * BLOCK))
        for h in range(N_KV_HEADS):
            kv_cols = slice(h * HEAD_DIM, (h + 1) * HEAD_DIM)
            k_parts = [kp_ref[:, kv_cols]] + [kc_ref[c * BLOCK:(c + 1) * BLOCK, kv_cols]
                                              for c in range(ATTN_Q_BLOCKS)] + [kn_ref[:, kv_cols]]
            v_parts = [vp_ref[:, kv_cols]] + [vc_ref[c * BLOCK:(c + 1) * BLOCK, kv_cols]
                                              for c in range(ATTN_Q_BLOCKS)] + [vn_ref[:, kv_cols]]
            kcat = jnp.concatenate(k_parts[b:b + 3], axis=0)
            vcat = jnp.concatenate(v_parts[b:b + 3], axis=0)
            for g in range(Q_PER_KV):
                head = h * Q_PER_KV + g
                cols = slice(head * HEAD_DIM, (head + 1) * HEAD_DIM)
                s = lax.dot_general(q_ref[q_rows, cols], kcat, (((1,), (1,)), ((), ())),
                                    preferred_element_type=F32)
                s = s - jnp.where(valid, _alibi_slope(head) * absrel_f, pos_inf)
                sink = sink_ref[head]
                m = jnp.maximum(jnp.max(s, axis=-1, keepdims=True), sink)
                p = jnp.exp(s - m)
                denom = jnp.sum(p, axis=-1, keepdims=True) + jnp.exp(sink - m)
                o = jnp.dot(p.astype(BF16), vcat, preferred_element_type=F32)
                o_ref[q_rows, cols] = (o / denom).astype(BF16)


def _attention(z, sink, seq_len):
    t = z.shape[0]
    nb = t // BLOCK
    qb = ATTN_Q_BLOCKS
    k_col = (IN_WIDTH - 2 * KV_WIDTH) // KV_WIDTH
    v_col = k_col + 1
    halo = lambda col: [pl.BlockSpec((BLOCK, KV_WIDTH), lambda n, s: (jnp.maximum(qb * n - 1, 0), col)),
                        pl.BlockSpec((qb * BLOCK, KV_WIDTH), lambda n, s: (n, col)),
                        pl.BlockSpec((BLOCK, KV_WIDTH), lambda n, s: (jnp.minimum(qb * n + qb, nb - 1), col))]
    grid_spec = pltpu.PrefetchScalarGridSpec(
        num_scalar_prefetch=1,
        grid=(nb // qb,),
        in_specs=[pl.BlockSpec((qb * BLOCK, D_MODEL), lambda n, s: (n, 0))] + halo(k_col) + halo(v_col),
        out_specs=pl.BlockSpec((qb * BLOCK, D_MODEL), lambda n, s: (n, 0)),
    )
    return pl.pallas_call(
        functools.partial(_attn_kernel, steps_per_seq=seq_len // (qb * BLOCK)),
        grid_spec=grid_spec,
        out_shape=jax.ShapeDtypeStruct((t, D_MODEL), BF16),
        compiler_params=_cparams(("arbitrary",)),
        name="window_attn",
    )(sink, z, z, z, z, z, z, z)


def _post_kernel(attn_ref, u_ref, vg_ref, ga_ref, gs_ref, x_ref, g1_ref, sh2_ref, sc2_ref, n2g_ref,
                 lng_ref, lnb_ref, ws_ref, bs_ref, woa_ref, wos_ref, wout_ref, wr_ref,
                 x1_ref, h2p_ref, aff_ref, afft_ref, sgu_ref):
    tm = x_ref.shape[0]
    vg = vg_ref[...].astype(F32)
    mu = jnp.mean(vg, axis=-1, keepdims=True)
    cen = vg - mu
    var = jnp.mean(cen * cen, axis=-1, keepdims=True)
    vn = (cen * lax.rsqrt(var + NORM_EPS) * lng_ref[...] + lnb_ref[...]).astype(BF16)
    for c in range(tm // BLOCK):
        rows = slice(c * BLOCK, (c + 1) * BLOCK)
        for g in range(N_SGU_GROUPS):
            cols = slice(g * BLOCK, (g + 1) * BLOCK)
            mixed = jnp.dot(ws_ref[g], vn[rows, cols], preferred_element_type=F32) + bs_ref[g]
            sgu_ref[rows, cols] = (u_ref[rows, cols].astype(F32) * mixed).astype(BF16)
    a = jnp.dot(attn_ref[...], woa_ref[...], preferred_element_type=F32)
    s = jnp.dot(sgu_ref[...], wos_ref[...], preferred_element_type=F32)
    merged = (ga_ref[...].astype(F32) * a + gs_ref[...].astype(F32) * s).astype(BF16)
    mix = jnp.dot(merged, wout_ref[...], preferred_element_type=F32)
    x1 = x_ref[...] + g1_ref[...] * mix
    x1_ref[...] = x1
    ms = jnp.mean(x1 * x1, axis=-1, keepdims=True)
    h2 = x1 * lax.rsqrt(ms + NORM_EPS) * n2g_ref[...]
    h2 = h2 * (1.0 + sc2_ref[...]) + sh2_ref[...]
    h2b = h2.astype(BF16)
    half = D_MODEL // 2
    n_slab = half // LANES
    lo = lax.shift_right_logical(lax.bitcast_convert_type(h2b[:, :half].astype(F32), I32), 16)
    hi = lax.bitcast_convert_type(h2b[:, half:].astype(F32), I32) & jnp.int32(-65536)
    word = hi | lo
    for j in range(n_slab):
        h2p_ref[pl.ds(j, tm, stride=n_slab), :] = word[:, j * LANES:(j + 1) * LANES]
    h_lo = (h2 - h2b.astype(F32)).astype(BF16)
    r1 = jnp.dot(h2b, wr_ref[...], preferred_element_type=F32)
    r2 = jnp.dot(h_lo, wr_ref[:, :LANES], preferred_element_type=F32)
    logits = r1[:, :LANES] + r1[:, LANES:] + r2
    lane = lax.broadcasted_iota(I32, logits.shape, 1)
    logits = jnp.where(lane < N_EXPERTS, logits, -jnp.inf)
    logits = logits - jnp.max(logits, axis=-1, keepdims=True)
    ex = jnp.exp(logits)
    aff = ex / jnp.sum(ex, axis=-1, keepdims=True)
    aff_ref[...] = aff[:, :N_EXPERTS]
    afft_ref[...] = aff.T[:N_EXPERTS, :]


def _post_mixer(attn_o, z, x2, mod3, norm2_g, ln_g, ln_b, ws, bs, w_oa, w_os, w_out, w_r2, seq_len, tm):
    t = x2.shape[0]
    n_slab = D_MODEL // 2 // LANES
    seq = lambda i: (i * tm) // seq_len
    const2 = lambda i: (0, 0)
    const3 = lambda i: (0, 0, 0)
    resident = lambda shape, imap: pl.BlockSpec(shape, imap, pipeline_mode=pl.Buffered(1))
    tok = lambda col: pl.BlockSpec((tm, D_MODEL), lambda i: (i, col))
    modv = lambda col: pl.BlockSpec((None, 1, D_MODEL), lambda i: (seq(i), 0, col))
    return pl.pallas_call(
        _post_kernel,
        grid=(t // tm,),
        in_specs=[tok(0),
                  tok(1), tok(2), tok(3), tok(4),
                  tok(0),
                  modv(2), modv(3), modv(4),
                  pl.BlockSpec((1, D_MODEL), const2),
                  pl.BlockSpec((1, D_MODEL), const2),
                  pl.BlockSpec((1, D_MODEL), const2),
                  resident((N_SGU_GROUPS, BLOCK, BLOCK), const3),
                  resident((N_SGU_GROUPS, BLOCK, BLOCK), const3),
                  resident((D_MODEL, D_MODEL), const2),
                  resident((D_MODEL, D_MODEL), const2),
                  resident((D_MODEL, D_MODEL), const2),
                  resident((D_MODEL, 2 * LANES), const2)],
        out_specs=[pl.BlockSpec((tm, D_MODEL), lambda i: (i, 0)),
                   pl.BlockSpec((tm * n_slab, LANES), lambda i: (i, 0)),
                   pl.BlockSpec((tm, N_EXPERTS), lambda i: (i, 0)),
                   pl.BlockSpec((N_EXPERTS, tm), lambda i: (0, i))],
        out_shape=[jax.ShapeDtypeStruct((t, D_MODEL), F32),
                   jax.ShapeDtypeStruct((t * n_slab, LANES), I32),
                   jax.ShapeDtypeStruct((t, N_EXPERTS), F32),
                   jax.ShapeDtypeStruct((N_EXPERTS, t), F32)],
        scratch_shapes=[pltpu.VMEM((tm, D_MODEL), BF16)],
        compiler_params=_cparams(("arbitrary",)),
        name="post_mixer",
    )(attn_o, z, z, z, z, x2, mod3, mod3, mod3, norm2_g, ln_g, ln_b, ws, bs, w_oa, w_os, w_out, w_r2)


def _route_kernel(aff_ref, idx_ref, tau_ref, need_ref, pref_ref, eqpref_ref, bits_ref, taus_ref, *, cap):
    n_exp, n_rows, _ = aff_ref.shape
    n_tok = n_rows * LANES
    bits_ref[...] = lax.bitcast_convert_type(aff_ref[...], I32)

    def bisect(i, v):
        cand = v | lax.shift_left(jnp.int32(1), 30 - i)
        ge = _ones_where(bits_ref[...] >= cand)
        cnt = jnp.sum(jnp.sum(ge, axis=1, keepdims=True), axis=2, keepdims=True)
        return jnp.where(cnt >= cap, cand, v)

    taus_ref[...] = lax.fori_loop(0, 31, bisect, jnp.zeros((n_exp, 1, LANES), I32))

    li = lax.broadcasted_iota(I32, (LANES, LANES), 0)
    lj = lax.broadcasted_iota(I32, (LANES, LANES), 1)
    upper_incl = _ones_where(li <= lj, BF16)
    ones_sq = jnp.ones((LANES, LANES), BF16)
    ri = lax.broadcasted_iota(I32, (n_rows, n_rows), 0)
    rj = lax.broadcasted_iota(I32, (n_rows, n_rows), 1)
    lower_strict = _ones_where(rj < ri, BF16)
    upper_strict = _ones_where(ri < rj, BF16)
    ones_rows = jnp.ones((SUBLANES, LANES), BF16)
    nt_dims = (((1,), (1,)), ((), ()))

    def incl_cumsum(mb):
        local = jnp.dot(mb, upper_incl, preferred_element_type=F32)
        totb = jnp.dot(mb, ones_sq, preferred_element_type=F32)
        prefc = jnp.dot(lower_strict, totb.astype(BF16), preferred_element_type=F32)
        return local + prefc

    def row_prefix(mb):
        tot_row = lax.dot_general(ones_rows, mb, nt_dims, preferred_element_type=F32)
        pref_row = jnp.dot(tot_row.astype(BF16), upper_strict, preferred_element_type=F32)
        return tot_row, pref_row

    def per_expert(e, carry):
        t = taus_ref[e]
        b = bits_ref[e]
        gt = b > t
        eq = b == t
        eqb = _ones_where(eq, BF16)
        need = cap - jnp.sum(_ones_where(gt))
        sel = gt | (eq & (incl_cumsum(eqb) <= need))
        m = _ones_where(sel, BF16)
        glob = incl_cumsum(m)
        tot_row, pref_row = row_prefix(m)
        incl_row = pref_row + tot_row
        _, eq_pref_row = row_prefix(eqb)
        pref_ref[e] = pref_row.astype(I32)
        eqpref_ref[e] = eq_pref_row
        tau_ref[e] = jnp.broadcast_to(lax.bitcast_convert_type(t, F32), (SUBLANES, LANES))
        need_ref[e] = jnp.full((SUBLANES, LANES), need, F32)
        ghi = jnp.floor(glob * (1.0 / MXU_DIM))
        glo = (glob - MXU_DIM * ghi).astype(BF16)
        ghi = ghi.astype(BF16)
        pr = pref_row[0:1, :]
        ir = incl_row[0:1, :]

        def chunk(c, carry2):
            s_r = (c * LANES + lax.broadcasted_iota(I32, (LANES, n_rows), 0)).astype(F32)
            onehot = _ones_where((pr <= s_r) & (s_r < ir), BF16)
            rowid = jnp.sum(_ones_where(ir <= s_r), axis=-1, keepdims=True)
            grow = (MXU_DIM * jnp.dot(onehot, ghi, preferred_element_type=F32)
                    + jnp.dot(onehot, glo, preferred_element_type=F32))
            s_l = (c * LANES + lax.broadcasted_iota(I32, (LANES, LANES), 0)).astype(F32)
            inrow = jnp.sum(_ones_where(grow <= s_l), axis=-1, keepdims=True)
            tok = jnp.minimum(rowid * LANES + inrow, n_tok - 1.0)
            tok_t = jnp.broadcast_to(tok, (LANES, LANES)).T
            idx_ref[e, pl.ds(c, 1), :] = tok_t[0:1, :].astype(I32)
            return carry2

        lax.fori_loop(0, cap // LANES, chunk, 0)
        return carry

    lax.fori_loop(0, n_exp, per_expert, 0)


def _route(aff3, cap):
    n_exp, n_rows, _ = aff3.shape
    rep = lambda dt, w: jax.ShapeDtypeStruct((n_exp, SUBLANES, w), dt)
    return pl.pallas_call(
        functools.partial(_route_kernel, cap=cap),
        out_shape=[jax.ShapeDtypeStruct((n_exp, cap // LANES, LANES), I32),
                   rep(F32, LANES), rep(F32, LANES), rep(I32, n_rows), rep(F32, n_rows)],
        scratch_shapes=[pltpu.VMEM((n_exp, n_rows, LANES), I32), pltpu.VMEM((n_exp, 1, LANES), I32)],
        compiler_params=pltpu.CompilerParams(vmem_limit_bytes=VMEM_LIMIT),
        name="ec_route",
    )(aff3)


def _ffn_kernel(idx_ref, h2p_hbm, wg_ref, wu_ref, wd_ref, o_ref, xraw_ref, xb_ref, acc_ref, sem, *,
                tc, rows_per_step, slab):
    ct_n = pl.num_programs(1)
    nf = pl.num_programs(2)
    f = pl.program_id(2)
    tile = pl.program_id(0) * ct_n + pl.program_id(1)
    n_tiles = pl.num_programs(0) * ct_n
    slot = tile % 2

    def row_copy(tile_id, s, dst_slot):
        tok = idx_ref[tile_id * tc + jnp.minimum(s, tc - 1)]
        return pltpu.make_async_copy(
            h2p_hbm.at[pl.ds(pl.multiple_of(tok * slab, slab), slab), :],
            xraw_ref.at[dst_slot, pl.ds(pl.multiple_of(s * slab, slab), slab), :], sem.at[dst_slot])

    def wait_slot(s_):
        pltpu.make_async_copy(xraw_ref.at[s_], xraw_ref.at[s_], sem.at[s_]).wait()

    @pl.when((f == 0) & (tile == 0))
    def _():
        def body(s, carry):
            row_copy(tile, s, slot).start()
            return carry
        lax.fori_loop(0, rows_per_step * nf, body, 0)

    @pl.when(f == 0)
    def _():
        wait_slot(slot)
        half = slab * LANES
        for j in range(slab):
            w = xraw_ref[slot, pl.ds(j, tc, stride=slab), :]
            lo = lax.bitcast_convert_type(lax.shift_left(w, 16), F32)
            hi = lax.bitcast_convert_type(w & jnp.int32(-65536), F32)
            xb_ref[:, j * LANES:(j + 1) * LANES] = lo.astype(BF16)
            xb_ref[:, half + j * LANES:half + (j + 1) * LANES] = hi.astype(BF16)
        acc_ref[...] = jnp.zeros_like(acc_ref)

    nxt = jnp.minimum(tile + 1, n_tiles - 1)
    for u in range(rows_per_step):
        row_copy(nxt, f * rows_per_step + u, 1 - slot).start()

    x = xb_ref[...]
    g = jnp.dot(x, wg_ref[...], preferred_element_type=F32)
    up = jnp.dot(x, wu_ref[...], preferred_element_type=F32)
    hmid = (g * _sigmoid(g) * up).astype(BF16)
    acc_ref[...] += jnp.dot(hmid, wd_ref[...], preferred_element_type=F32)

    @pl.when(f == nf - 1)
    def _():
        o_ref[...] = acc_ref[...].astype(BF16)

    @pl.when((f == nf - 1) & (tile == n_tiles - 1))
    def _():
        wait_slot(1 - slot)


def _expert_ffn(idx_flat, h2p, w_gate, w_up, w_down, cap, tc, fc):
    n_exp, d, d_ff = w_gate.shape
    slab = d // 2 // LANES
    nct = cap // tc
    nf = d_ff // fc
    rows_per_step = -(-tc // nf)
    grid_spec = pltpu.PrefetchScalarGridSpec(
        num_scalar_prefetch=1,
        grid=(n_exp, nct, nf),
        in_specs=[pl.BlockSpec(memory_space=pl.ANY),
                  pl.BlockSpec((None, d, fc), lambda ei, ci, fi, idx: (ei, 0, fi)),
                  pl.BlockSpec((None, d, fc), lambda ei, ci, fi, idx: (ei, 0, fi)),
                  pl.BlockSpec((None, fc, d), lambda ei, ci, fi, idx: (ei, fi, 0))],
        out_specs=pl.BlockSpec((tc, d), lambda ei, ci, fi, idx: (ei * nct + ci, 0)),
        scratch_shapes=[pltpu.VMEM((2, rows_per_step * nf * slab, LANES), I32),
                        pltpu.VMEM((tc, d), BF16),
                        pltpu.VMEM((tc, d), F32),
                        pltpu.SemaphoreType.DMA((2,))],
    )
    return pl.pallas_call(
        functools.partial(_ffn_kernel, tc=tc, rows_per_step=rows_per_step, slab=slab),
        grid_spec=grid_spec,
        out_shape=jax.ShapeDtypeStruct((n_exp * cap, d), BF16),
        compiler_params=_cparams(("arbitrary", "arbitrary", "arbitrary")),
        name="expert_ffn",
    )(idx_flat, h2p, w_gate, w_up, w_down)


def _combine_kernel(tab_ref, aff_ref, tau_ref, need_ref, eqs_ref, x1_ref, g2_ref, y_hbm, o_ref,
                    ybuf_ref, acc_ref, sem, *, cap, n_tile):
    i = pl.program_id(0)
    tm, n_exp = aff_ref.shape
    slot_rows = COMBINE_SLOT
    k_rows = n_exp * slot_rows
    total_rows = n_exp * cap
    lane_e = lax.broadcasted_iota(I32, (1, n_exp), 1)

    def geometry(tile, e):
        n0 = tab_ref[e * (n_tile + 1) + tile]
        n1 = tab_ref[e * (n_tile + 1) + tile + 1]
        first = e * cap + n0
        aligned = (first // BF16_ROWS) * BF16_ROWS
        return aligned, first - aligned, n1 - n0

    def round_src(aligned, q):
        src = aligned + q * slot_rows
        clamped = jnp.minimum(src, total_rows - slot_rows)
        return clamped, src - clamped

    def issue_round(tile, q, buf):
        for e in range(n_exp):
            aligned, _, _ = geometry(tile, e)
            src, _ = round_src(aligned, q)
            pltpu.make_async_copy(y_hbm.at[pl.ds(pl.multiple_of(src, BF16_ROWS), slot_rows), :],
                                  ybuf_ref.at[buf, pl.ds(e * slot_rows, slot_rows), :], sem.at[buf]).start()

    buf = i % 2

    @pl.when(i == 0)
    def _():
        issue_round(i, 0, buf)

    @pl.when(i + 1 < n_tile)
    def _():
        issue_round(i + 1, 0, 1 - buf)

    a = aff_ref[...]
    tau = tau_ref[...]
    eq = a == tau
    ti = lax.broadcasted_iota(I32, (tm, tm), 0)
    tj = lax.broadcasted_iota(I32, (tm, tm), 1)
    eq_rank = eqs_ref[...] + jnp.dot(_ones_where(tj <= ti, BF16), _ones_where(eq, BF16),
                                     preferred_element_type=F32)
    sel = (a > tau) | (eq & (eq_rank <= need_ref[...]))
    wm = jnp.where(sel, a, 0.0).astype(BF16)
    rank = jnp.dot(_ones_where(tj < ti, BF16), _ones_where(sel, BF16), preferred_element_type=F32)

    n_round = jnp.int32(1)
    for e in range(n_exp):
        _, delta, n_sel = geometry(i, e)
        n_round = jnp.maximum(n_round, (delta + n_sel + slot_rows - 1) // slot_rows)

    spread = _ones_where(lax.broadcasted_iota(I32, (n_exp, k_rows), 1) // slot_rows
                         == lax.broadcasted_iota(I32, (n_exp, k_rows), 0), BF16)
    col_in_slot = (lax.broadcasted_iota(I32, (tm, k_rows), 1) % slot_rows).astype(F32)
    wm_cols = jnp.dot(wm, spread, preferred_element_type=F32)
    acc_ref[...] = jnp.zeros_like(acc_ref)

    def one_round(q, carry):
        @pl.when(q > 0)
        def _():
            issue_round(i, q, buf)

        pltpu.make_async_copy(ybuf_ref.at[buf], ybuf_ref.at[buf], sem.at[buf]).wait()
        offset = jnp.zeros((1, n_exp), F32)
        shift = jnp.zeros((1, n_exp), F32)
        for e in range(n_exp):
            aligned, delta, _ = geometry(i, e)
            _, sh = round_src(aligned, q)
            offset = jnp.where(lane_e == e, (delta - q * slot_rows).astype(F32), offset)
            shift = jnp.where(lane_e == e, sh.astype(F32), shift)
        u = rank + offset
        pos = jnp.where(u >= 0, u + shift, -1.0).astype(BF16)
        pos_cols = jnp.dot(pos, spread, preferred_element_type=F32)
        place = jnp.where(pos_cols == col_in_slot, wm_cols, 0.0).astype(BF16)
        acc_ref[...] += jnp.dot(place, ybuf_ref[buf], preferred_element_type=F32)
        return carry

    lax.fori_loop(0, n_round, one_round, 0)
    o_ref[...] = x1_ref[...] + g2_ref[...] * acc_ref[...]


def _combine(tab, aff, tau, need, eqs, x1, mod3, y, cap, seq_len, tm):
    t, d = x1.shape
    n_exp = aff.shape[1]
    n_tile = t // tm
    seq = lambda i, tab_: ((i * tm) // seq_len, 0, 5)
    grid_spec = pltpu.PrefetchScalarGridSpec(
        num_scalar_prefetch=1,
        grid=(n_tile,),
        in_specs=[pl.BlockSpec((tm, n_exp), lambda i, tab_: (i, 0)),
                  pl.BlockSpec((1, n_exp), lambda i, tab_: (0, 0)),
                  pl.BlockSpec((1, n_exp), lambda i, tab_: (0, 0)),
                  pl.BlockSpec((None, 1, n_exp), lambda i, tab_: (i, 0, 0)),
                  pl.BlockSpec((tm, d), lambda i, tab_: (i, 0)),
                  pl.BlockSpec((None, 1, d), seq),
                  pl.BlockSpec(memory_space=pl.ANY)],
        out_specs=pl.BlockSpec((tm, d), lambda i, tab_: (i, 0)),
        scratch_shapes=[pltpu.VMEM((2, n_exp * COMBINE_SLOT, d), BF16),
                        pltpu.VMEM((tm, d), F32),
                        pltpu.SemaphoreType.DMA((2,))],
    )
    return pl.pallas_call(
        functools.partial(_combine_kernel, cap=cap, n_tile=n_tile),
        grid_spec=grid_spec,
        out_shape=jax.ShapeDtypeStruct((t, d), F32),
        compiler_params=_cparams(("arbitrary",)),
        name="ec_combine",
    )(tab, aff, tau, need, eqs, x1, mod3, y)


def _moe(h2p, aff, afft, x1, mod_g, w_gate, w_up, w_down, seq_len):
    t = aff.shape[0]
    n_exp = aff.shape[1]
    cap = CAPACITY_FACTOR * t // n_exp
    tm = COMBINE_TILE
    idx, tau, need, pref, eqpref = _route(afft.reshape(n_exp, t // LANES, LANES), cap)
    y = _expert_ffn(idx.reshape(-1), h2p, w_gate, w_up, w_down, cap, min(1024, cap), 256)
    rows_per_tile = tm // LANES
    tab = jnp.concatenate([pref[:, 0, ::rows_per_tile], jnp.full((n_exp, 1), cap, I32)], axis=1).reshape(-1)
    eqs = eqpref[:, 0, ::rows_per_tile].T.reshape(t // tm, 1, n_exp)
    return _combine(tab, aff, tau[:, 0, 0].reshape(1, n_exp), need[:, 0, 0].reshape(1, n_exp), eqs,
                    x1, mod_g, y, cap, seq_len, tm)


def _trunk(x, mod3, seq_base, prm):
    b, s, d = x.shape
    t = b * s
    x2 = x.reshape(t, d)
    mod_g = lax.slice_in_dim(mod3, seq_base, seq_base + b, axis=0)
    z = _in_projection(x2, mod_g, prm["norm1_g"], prm["w_in"], prm["q_g"], prm["k_g"], s, min(1024, s))
    attn_o = _attention(z, prm["sink"], s)
    x1, h2p, aff, afft = _post_mixer(attn_o, z, x2, mod_g, prm["norm2_g"], prm["ln_g"], prm["ln_b"], prm["ws"],
                                     prm["bs"], prm["w_oa"], prm["w_os"], prm["w_out"], prm["w_r2"], s, 256)
    out = _moe(h2p, aff, afft, x1, mod_g, prm["w_gate"], prm["w_up"], prm["w_down"], s)
    return out.reshape(b, s, d)


def kernel(x_prompt, x_sample, c_prompt, c_sample, w_ada, b_ada, norm1_g, norm2_g, w_in, q_norm_g, k_norm_g,
           attn_sink, sgu_ln_g, sgu_ln_b, sgu_w, sgu_b, w_o_attn, w_o_sgu, w_out, w_router, w_gate, w_up,
           w_down):
    assert w_ada.shape[0] == 1
    bp, bs_ = x_prompt.shape[0], x_sample.shape[0]
    assert bp + bs_ <= N_SEQ_PAD
    d = D_MODEL
    c_pad = jnp.zeros((N_SEQ_PAD, d), F32).at[:bp].set(c_prompt).at[bp:bp + bs_].set(c_sample)
    mod = _modulation(c_pad, w_ada[0], b_ada[0])
    mod3 = mod.reshape(N_SEQ_PAD, 1, 6 * d)

    w = w_in[0]
    q_end = N_Q_HEADS * HEAD_DIM
    kv_end = q_end + 2 * KV_WIDTH
    w_in_r = jnp.concatenate([w[:, :q_end], w[:, kv_end:], w[:, q_end:kv_end]], axis=1).astype(BF16)
    w_r = w_router[0]
    w_r_hi = w_r.astype(BF16)
    w_r_lo = (w_r - w_r_hi.astype(F32)).astype(BF16)
    pad = ((0, 0), (0, LANES - N_EXPERTS))
    w_r2 = jnp.concatenate([jnp.pad(w_r_hi, pad), jnp.pad(w_r_lo, pad)], axis=1)
    prm = dict(
        norm1_g=norm1_g[0].reshape(1, d), norm2_g=norm2_g[0].reshape(1, d), w_in=w_in_r,
        q_g=q_norm_g[0].reshape(1, HEAD_DIM), k_g=k_norm_g[0].reshape(1, HEAD_DIM), sink=attn_sink[0],
        ln_g=sgu_ln_g[0].reshape(1, d), ln_b=sgu_ln_b[0].reshape(1, d),
        ws=sgu_w[0].astype(BF16),
        bs=jnp.broadcast_to(sgu_b[0][:, :, None], (N_SGU_GROUPS, BLOCK, BLOCK)),
        w_oa=w_o_attn[0].astype(BF16), w_os=w_o_sgu[0].astype(BF16), w_out=w_out[0].astype(BF16),
        w_r2=w_r2, w_gate=w_gate[0].astype(BF16), w_up=w_up[0].astype(BF16),
        w_down=w_down[0].astype(BF16))
    y_prompt = _trunk(x_prompt, mod3, 0, prm)
    y_sample = _trunk(x_sample, mod3, bp, prm)
    return (y_prompt, y_sample)
```

```python
import functools

import jax
import jax.numpy as jnp
from jax import lax
from jax.experimental import pallas as pl
from jax.experimental.pallas import tpu as pltpu

F32 = jnp.float32
BF16 = jnp.bfloat16
I32 = jnp.int32

D_MODEL = 2048
HEAD_DIM = 128
N_Q_HEADS = 16
N_KV_HEADS = 4
Q_PER_KV = N_Q_HEADS // N_KV_HEADS
KV_WIDTH = N_KV_HEADS * HEAD_DIM
BLOCK = 128
N_SGU_GROUPS = 16
N_EXPERTS = 16
CAPACITY_FACTOR = 2
D_FF = 2816
NORM_EPS = 1e-6
IN_WIDTH = 11264
N_SEQ_PAD = 8

LANES = 128
SUBLANES = 8
BF16_ROWS = 16
MXU_DIM = 256

COL_TILE = 1024
Q_TILES = (0, 2)
GELU_TILES = (2, 6)
SIG_TILES = (6, 10)
KV_TILE = 10
N_COL_TILES = IN_WIDTH // COL_TILE

INPROJ_ROW_CHUNK = 256
ATTN_Q_BLOCKS = 2
COMBINE_TILE = 256
COMBINE_SLOT = 64

VMEM_LIMIT = 56 * 1024 * 1024


def _cparams(sem):
    return pltpu.CompilerParams(dimension_semantics=sem, vmem_limit_bytes=VMEM_LIMIT)


def _gelu_tanh(x):
    c = 0.7978845608028654
    return 0.5 * x * (1.0 + jnp.tanh(c * (x + 0.044715 * (x * x * x))))


def _sigmoid(x):
    return 1.0 / (1.0 + jnp.exp(-x))


def _ones_where(cond, dtype=F32):
    return jnp.where(cond, 1.0, 0.0).astype(dtype)


def _mod_kernel(c_ref, w_ref, b_ref, o_ref):
    c = c_ref[...]
    s = c * _sigmoid(c)
    o_ref[...] = jnp.dot(s, w_ref[...], precision=lax.Precision.HIGHEST,
                         preferred_element_type=F32) + b_ref[...]


def _modulation(c_pad, w_ada, b_ada):
    n = w_ada.shape[1]
    tn = 1024
    return pl.pallas_call(
        _mod_kernel,
        grid=(n // tn,),
        in_specs=[pl.BlockSpec((N_SEQ_PAD, D_MODEL), lambda j: (0, 0)),
                  pl.BlockSpec((D_MODEL, tn), lambda j: (0, j)),
                  pl.BlockSpec((1, tn), lambda j: (0, j))],
        out_specs=pl.BlockSpec((N_SEQ_PAD, tn), lambda j: (0, j)),
        out_shape=jax.ShapeDtypeStruct((N_SEQ_PAD, n), F32),
        compiler_params=_cparams(("arbitrary",)),
        name="adaln_mod",
    )(c_pad, w_ada, b_ada.reshape(1, n))


def _inproj_kernel(x_ref, sh_ref, sc_ref, g_ref, w_ref, qg_ref, kg_ref, o_ref, h_ref):
    j = pl.program_id(1)

    @pl.when(j == 0)
    def _():
        x = x_ref[...]
        ms = jnp.mean(x * x, axis=-1, keepdims=True)
        y = x * lax.rsqrt(ms + NORM_EPS) * g_ref[...]
        h_ref[...] = (y * (1.0 + sc_ref[...]) + sh_ref[...]).astype(BF16)

    def head_norm(sub, g):
        ms = jnp.mean(sub * sub, axis=-1, keepdims=True)
        return sub * lax.rsqrt(ms + NORM_EPS) * g

    def chunked(epilogue):
        for r in range(x_ref.shape[0] // INPROJ_ROW_CHUNK):
            rows = pl.ds(r * INPROJ_ROW_CHUNK, INPROJ_ROW_CHUNK)
            epilogue(rows, jnp.dot(h_ref[rows, :], w_ref[...], preferred_element_type=F32))

    @pl.when(j < Q_TILES[1])
    def _():
        g = qg_ref[...] * (HEAD_DIM ** -0.5)

        def epilogue(rows, acc):
            for hh in range(COL_TILE // HEAD_DIM):
                cols = slice(hh * HEAD_DIM, (hh + 1) * HEAD_DIM)
                o_ref[rows, cols] = head_norm(acc[:, cols], g).astype(BF16)
        chunked(epilogue)

    @pl.when((j >= GELU_TILES[0]) & (j < GELU_TILES[1]))
    def _():
        def epilogue(rows, acc):
            o_ref[rows, :] = _gelu_tanh(acc).astype(BF16)
        chunked(epilogue)

    @pl.when((j >= SIG_TILES[0]) & (j < SIG_TILES[1]))
    def _():
        def epilogue(rows, acc):
            o_ref[rows, :] = _sigmoid(acc).astype(BF16)
        chunked(epilogue)

    @pl.when(j == KV_TILE)
    def _():
        g = kg_ref[...]

        def epilogue(rows, acc):
            for hh in range(N_KV_HEADS):
                cols = slice(hh * HEAD_DIM, (hh + 1) * HEAD_DIM)
                o_ref[rows, cols] = head_norm(acc[:, cols], g).astype(BF16)
            o_ref[rows, KV_WIDTH:] = acc[:, KV_WIDTH:].astype(BF16)
        chunked(epilogue)


def _in_projection(x2, mod3, norm_g, w_in_r, q_g, k_g, seq_len, tm):
    t = x2.shape[0]
    seq = lambda i: (i * tm) // seq_len
    return pl.pallas_call(
        _inproj_kernel,
        grid=(t // tm, N_COL_TILES),
        in_specs=[pl.BlockSpec((tm, D_MODEL), lambda i, j: (i, 0)),
                  pl.BlockSpec((None, 1, D_MODEL), lambda i, j: (seq(i), 0, 0)),
                  pl.BlockSpec((None, 1, D_MODEL), lambda i, j: (seq(i), 0, 1)),
                  pl.BlockSpec((1, D_MODEL), lambda i, j: (0, 0)),
                  pl.BlockSpec((D_MODEL, COL_TILE), lambda i, j: (0, j)),
                  pl.BlockSpec((1, HEAD_DIM), lambda i, j: (0, 0)),
                  pl.BlockSpec((1, HEAD_DIM), lambda i, j: (0, 0))],
        out_specs=pl.BlockSpec((tm, COL_TILE), lambda i, j: (i, j)),
        out_shape=jax.ShapeDtypeStruct((t, IN_WIDTH), BF16),
        scratch_shapes=[pltpu.VMEM((tm, D_MODEL), BF16)],
        compiler_params=_cparams(("arbitrary", "arbitrary")),
        name="norm_inproj",
    )(x2, mod3, mod3, norm_g, w_in_r, q_g, k_g)


def _alibi_slope(head):
    return 2.0 ** (-8.0 * (head + 1) / N_Q_HEADS)


def _attn_kernel(sink_ref, q_ref, kp_ref, kc_ref, kn_ref, vp_ref, vc_ref, vn_ref, o_ref, *, steps_per_seq):
    n = pl.program_id(0)
    pos = n % steps_per_seq
    first = pos == 0
    last = pos == steps_per_seq - 1

    qi = lax.broadcasted_iota(I32, (BLOCK, 3 * BLOCK), 0)
    kj = lax.broadcasted_iota(I32, (BLOCK, 3 * BLOCK), 1)
    absrel = jnp.abs(kj - BLOCK - qi)
    in_window = absrel <= BLOCK
    absrel_f = absrel.astype(F32)
    pos_inf = jnp.float32(jnp.inf)

    for b in range(ATTN_Q_BLOCKS):
        q_rows = slice(b * BLOCK, (b + 1) * BLOCK)
        valid = in_window
        if b == 0:
            valid = valid & jnp.logical_not(first & (kj < BLOCK))
        if b == ATTN_Q_BLOCKS - 1:
            valid = valid & jnp.logical_not(last & (kj >= 2 * BLOCK))
        for h in range(N_KV_HEADS):
            kv_cols = slice(h * HEAD_DIM, (h + 1) * HEAD_DIM)
            k_parts = [kp_ref[:, kv_cols]] + [kc_ref[c * BLOCK:(c + 1) * BLOCK, kv_cols]
                                              for c in range(ATTN_Q_BLOCKS)] + [kn_ref[:, kv_cols]]
            v_parts = [vp_ref[:, kv_cols]] + [vc_ref[c * BLOCK:(c + 1) * BLOCK, kv_cols]
                                              for c in range(ATTN_Q_BLOCKS)] + [vn_ref[:, kv_cols]]
            kcat = jnp.concatenate(k_parts[b:b + 3], axis=0)
            vcat = jnp.concatenate(v_parts[b:b + 3], axis=0)
            heads = [h * Q_PER_KV + g for g in range(Q_PER_KV)]
            cols = [slice(hd * HEAD_DIM, (hd + 1) * HEAD_DIM) for hd in heads]
            sinks = [sink_ref[hd] for hd in heads]
            s = [lax.dot_general(q_ref[q_rows, c], kcat, (((1,), (1,)), ((), ())), preferred_element_type=F32)
                 for c in cols]
            s = [sg - jnp.where(valid, _alibi_slope(hd) * absrel_f, pos_inf) for sg, hd in zip(s, heads)]
            m = [jnp.maximum(jnp.max(sg, axis=-1, keepdims=True), sk) for sg, sk in zip(s, sinks)]
            p = [jnp.exp(sg - mg) for sg, mg in zip(s, m)]
            denom = [jnp.sum(pg, axis=-1, keepdims=True) + jnp.exp(sk - mg) for pg, sk, mg in zip(p, sinks, m)]
            o = [jnp.dot(pg.astype(BF16), vcat, preferred_element_type=F32) for pg in p]
            for c, og, dg in zip(cols, o, denom):
                o_ref[q_rows, c] = (og / dg).astype(BF16)


def _attention(z, sink, seq_len):
    t = z.shape[0]
    nb = t // BLOCK
    qb = ATTN_Q_BLOCKS
    k_col = (IN_WIDTH - 2 * KV_WIDTH) // KV_WIDTH
    v_col = k_col + 1
    halo = lambda col: [pl.BlockSpec((BLOCK, KV_WIDTH), lambda n, s: (jnp.maximum(qb * n - 1, 0), col)),
                        pl.BlockSpec((qb * BLOCK, KV_WIDTH), lambda n, s: (n, col)),
                        pl.BlockSpec((BLOCK, KV_WIDTH), lambda n, s: (jnp.minimum(qb * n + qb, nb - 1), col))]
    grid_spec = pltpu.PrefetchScalarGridSpec(
        num_scalar_prefetch=1,
        grid=(nb // qb,),
        in_specs=[pl.BlockSpec((qb * BLOCK, D_MODEL), lambda n, s: (n, 0))] + halo(k_col) + halo(v_col),
        out_specs=pl.BlockSpec((qb * BLOCK, D_MODEL), lambda n, s: (n, 0)),
    )
    return pl.pallas_call(
        functools.partial(_attn_kernel, steps_per_seq=seq_len // (qb * BLOCK)),
        grid_spec=grid_spec,
        out_shape=jax.ShapeDtypeStruct((t, D_MODEL), BF16),
        compiler_params=_cparams(("arbitrary",)),
        name="window_attn",
    )(sink, z, z, z, z, z, z, z)


def _post_kernel(attn_ref, u_ref, vg_ref, ga_ref, gs_ref, x_ref, g1_ref, sh2_ref, sc2_ref, n2g_ref,
                 lng_ref, lnb_ref, ws_ref, bs_ref, woa_ref, wos_ref, wout_ref, wr_ref,
                 x1_ref, h2p_ref, aff_ref, afft_ref, sgu_ref):
    tm = x_ref.shape[0]
    vg = vg_ref[...].astype(F32)
    mu = jnp.mean(vg, axis=-1, keepdims=True)
    cen = vg - mu
    var = jnp.mean(cen * cen, axis=-1, keepdims=True)
    vn = (cen * lax.rsqrt(var + NORM_EPS) * lng_ref[...] + lnb_ref[...]).astype(BF16)
    for c in range(tm // BLOCK):
        rows = slice(c * BLOCK, (c + 1) * BLOCK)
        for g in range(N_SGU_GROUPS):
            cols = slice(g * BLOCK, (g + 1) * BLOCK)
            mixed = jnp.dot(ws_ref[g], vn[rows, cols], preferred_element_type=F32) + bs_ref[g]
            sgu_ref[rows, cols] = (u_ref[rows, cols].astype(F32) * mixed).astype(BF16)
    a = jnp.dot(attn_ref[...], woa_ref[...], preferred_element_type=F32)
    s = jnp.dot(sgu_ref[...], wos_ref[...], preferred_element_type=F32)
    merged = (ga_ref[...].astype(F32) * a + gs_ref[...].astype(F32) * s).astype(BF16)
    mix = jnp.dot(merged, wout_ref[...], preferred_element_type=F32)
    x1 = x_ref[...] + g1_ref[...] * mix
    x1_ref[...] = x1
    ms = jnp.mean(x1 * x1, axis=-1, keepdims=True)
    h2 = x1 * lax.rsqrt(ms + NORM_EPS) * n2g_ref[...]
    h2 = h2 * (1.0 + sc2_ref[...]) + sh2_ref[...]
    h2b = h2.astype(BF16)
    half = D_MODEL // 2
    n_slab = half // LANES
    lo = lax.shift_right_logical(lax.bitcast_convert_type(h2b[:, :half].astype(F32), I32), 16)
    hi = lax.bitcast_convert_type(h2b[:, half:].astype(F32), I32) & jnp.int32(-65536)
    word = hi | lo
    for j in range(n_slab):
        h2p_ref[pl.ds(j, tm, stride=n_slab), :] = word[:, j * LANES:(j + 1) * LANES]
    h_lo = (h2 - h2b.astype(F32)).astype(BF16)
    r1 = jnp.dot(h2b, wr_ref[...], preferred_element_type=F32)
    r2 = jnp.dot(h_lo, wr_ref[:, :LANES], preferred_element_type=F32)
    logits = r1[:, :LANES] + r1[:, LANES:] + r2
    lane = lax.broadcasted_iota(I32, logits.shape, 1)
    logits = jnp.where(lane < N_EXPERTS, logits, -jnp.inf)
    logits = logits - jnp.max(logits, axis=-1, keepdims=True)
    ex = jnp.exp(logits)
    aff = ex / jnp.sum(ex, axis=-1, keepdims=True)
    aff_ref[...] = aff[:, :N_EXPERTS]
    afft_ref[...] = aff.T[:N_EXPERTS, :]


def _post_mixer(attn_o, z, x2, mod3, norm2_g, ln_g, ln_b, ws, bs, w_oa, w_os, w_out, w_r2, seq_len, tm):
    t = x2.shape[0]
    n_slab = D_MODEL // 2 // LANES
    seq = lambda i: (i * tm) // seq_len
    const2 = lambda i: (0, 0)
    const3 = lambda i: (0, 0, 0)
    resident = lambda shape, imap: pl.BlockSpec(shape, imap, pipeline_mode=pl.Buffered(1))
    tok = lambda col: pl.BlockSpec((tm, D_MODEL), lambda i: (i, col))
    modv = lambda col: pl.BlockSpec((None, 1, D_MODEL), lambda i: (seq(i), 0, col))
    return pl.pallas_call(
        _post_kernel,
        grid=(t // tm,),
        in_specs=[tok(0),
                  tok(1), tok(2), tok(3), tok(4),
                  tok(0),
                  modv(2), modv(3), modv(4),
                  pl.BlockSpec((1, D_MODEL), const2),
                  pl.BlockSpec((1, D_MODEL), const2),
                  pl.BlockSpec((1, D_MODEL), const2),
                  resident((N_SGU_GROUPS, BLOCK, BLOCK), const3),
                  resident((N_SGU_GROUPS, BLOCK, BLOCK), const3),
                  resident((D_MODEL, D_MODEL), const2),
                  resident((D_MODEL, D_MODEL), const2),
                  resident((D_MODEL, D_MODEL), const2),
                  resident((D_MODEL, 2 * LANES), const2)],
        out_specs=[pl.BlockSpec((tm, D_MODEL), lambda i: (i, 0)),
                   pl.BlockSpec((tm * n_slab, LANES), lambda i: (i, 0)),
                   pl.BlockSpec((tm, N_EXPERTS), lambda i: (i, 0)),
                   pl.BlockSpec((N_EXPERTS, tm), lambda i: (0, i))],
        out_shape=[jax.ShapeDtypeStruct((t, D_MODEL), F32),
                   jax.ShapeDtypeStruct((t * n_slab, LANES), I32),
                   jax.ShapeDtypeStruct((t, N_EXPERTS), F32),
                   jax.ShapeDtypeStruct((N_EXPERTS, t), F32)],
        scratch_shapes=[pltpu.VMEM((tm, D_MODEL), BF16)],
        compiler_params=_cparams(("arbitrary",)),
        name="post_mixer",
    )(attn_o, z, z, z, z, x2, mod3, mod3, mod3, norm2_g, ln_g, ln_b, ws, bs, w_oa, w_os, w_out, w_r2)


def _route_kernel(aff_ref, idx_ref, tau_ref, need_ref, pref_ref, eqpref_ref, bits_ref, taus_ref, *, cap):
    n_exp, n_rows, _ = aff_ref.shape
    n_tok = n_rows * LANES
    bits_ref[...] = lax.bitcast_convert_type(aff_ref[...], I32)

    def bisect(i, v):
        cand = v | lax.shift_left(jnp.int32(1), 30 - i)
        ge = _ones_where(bits_ref[...] >= cand)
        cnt = jnp.sum(jnp.sum(ge, axis=1, keepdims=True), axis=2, keepdims=True)
        return jnp.where(cnt >= cap, cand, v)

    taus_ref[...] = lax.fori_loop(0, 31, bisect, jnp.zeros((n_exp, 1, LANES), I32))

    li = lax.broadcasted_iota(I32, (LANES, LANES), 0)
    lj = lax.broadcasted_iota(I32, (LANES, LANES), 1)
    upper_incl = _ones_where(li <= lj, BF16)
    ones_sq = jnp.ones((LANES, LANES), BF16)
    ri = lax.broadcasted_iota(I32, (n_rows, n_rows), 0)
    rj = lax.broadcasted_iota(I32, (n_rows, n_rows), 1)
    lower_strict = _ones_where(rj < ri, BF16)
    upper_strict = _ones_where(ri < rj, BF16)
    ones_rows = jnp.ones((SUBLANES, LANES), BF16)
    nt_dims = (((1,), (1,)), ((), ()))

    def incl_cumsum(mb):
        local = jnp.dot(mb, upper_incl, preferred_element_type=F32)
        totb = jnp.dot(mb, ones_sq, preferred_element_type=F32)
        prefc = jnp.dot(lower_strict, totb.astype(BF16), preferred_element_type=F32)
        return local + prefc

    def row_prefix(mb):
        tot_row = lax.dot_general(ones_rows, mb, nt_dims, preferred_element_type=F32)
        pref_row = jnp.dot(tot_row.astype(BF16), upper_strict, preferred_element_type=F32)
        return tot_row, pref_row

    def per_expert(e, carry):
        t = taus_ref[e]
        b = bits_ref[e]
        gt = b > t
        eq = b == t
        eqb = _ones_where(eq, BF16)
        need = cap - jnp.sum(_ones_where(gt))
        sel = gt | (eq & (incl_cumsum(eqb) <= need))
        m = _ones_where(sel, BF16)
        glob = incl_cumsum(m)
        tot_row, pref_row = row_prefix(m)
        incl_row = pref_row + tot_row
        _, eq_pref_row = row_prefix(eqb)
        pref_ref[e] = pref_row.astype(I32)
        eqpref_ref[e] = eq_pref_row
        tau_ref[e] = jnp.broadcast_to(lax.bitcast_convert_type(t, F32), (SUBLANES, LANES))
        need_ref[e] = jnp.full((SUBLANES, LANES), need, F32)
        ghi = jnp.floor(glob * (1.0 / MXU_DIM))
        glo = (glob - MXU_DIM * ghi).astype(BF16)
        ghi = ghi.astype(BF16)
        pr = pref_row[0:1, :]
        ir = incl_row[0:1, :]

        def chunk(c, carry2):
            s_r = (c * LANES + lax.broadcasted_iota(I32, (LANES, n_rows), 0)).astype(F32)
            onehot = _ones_where((pr <= s_r) & (s_r < ir), BF16)
            rowid = jnp.sum(_ones_where(ir <= s_r), axis=-1, keepdims=True)
            grow = (MXU_DIM * jnp.dot(onehot, ghi, preferred_element_type=F32)
                    + jnp.dot(onehot, glo, preferred_element_type=F32))
            s_l = (c * LANES + lax.broadcasted_iota(I32, (LANES, LANES), 0)).astype(F32)
            inrow = jnp.sum(_ones_where(grow <= s_l), axis=-1, keepdims=True)
            tok = jnp.minimum(rowid * LANES + inrow, n_tok - 1.0)
            tok_t = jnp.broadcast_to(tok, (LANES, LANES)).T
            idx_ref[e, pl.ds(c, 1), :] = tok_t[0:1, :].astype(I32)
            return carry2

        lax.fori_loop(0, cap // LANES, chunk, 0)
        return carry

    lax.fori_loop(0, n_exp, per_expert, 0)


def _route(aff3, cap):
    n_exp, n_rows, _ = aff3.shape
    rep = lambda dt, w: jax.ShapeDtypeStruct((n_exp, SUBLANES, w), dt)
    return pl.pallas_call(
        functools.partial(_route_kernel, cap=cap),
        out_shape=[jax.ShapeDtypeStruct((n_exp, cap // LANES, LANES), I32),
                   rep(F32, LANES), rep(F32, LANES), rep(I32, n_rows), rep(F32, n_rows)],
        scratch_shapes=[pltpu.VMEM((n_exp, n_rows, LANES), I32), pltpu.VMEM((n_exp, 1, LANES), I32)],
        compiler_params=pltpu.CompilerParams(vmem_limit_bytes=VMEM_LIMIT),
        name="ec_route",
    )(aff3)


def _ffn_kernel(idx_ref, h2p_hbm, wg_ref, wu_ref, wd_ref, o_ref, xraw_ref, xb_ref, acc_ref, sem, *,
                tc, rows_per_step, slab):
    ct_n = pl.num_programs(1)
    nf = pl.num_programs(2)
    f = pl.program_id(2)
    tile = pl.program_id(0) * ct_n + pl.program_id(1)
    n_tiles = pl.num_programs(0) * ct_n
    slot = tile % 2

    def row_copy(tile_id, s, dst_slot):
        tok = idx_ref[tile_id * tc + jnp.minimum(s, tc - 1)]
        return pltpu.make_async_copy(
            h2p_hbm.at[pl.ds(pl.multiple_of(tok * slab, slab), slab), :],
            xraw_ref.at[dst_slot, pl.ds(pl.multiple_of(s * slab, slab), slab), :], sem.at[dst_slot])

    def wait_slot(s_):
        pltpu.make_async_copy(xraw_ref.at[s_], xraw_ref.at[s_], sem.at[s_]).wait()

    @pl.when((f == 0) & (tile == 0))
    def _():
        def body(s, carry):
            row_copy(tile, s, slot).start()
            return carry
        lax.fori_loop(0, rows_per_step * nf, body, 0)

    @pl.when(f == 0)
    def _():
        wait_slot(slot)
        half = slab * LANES
        for j in range(slab):
            w = xraw_ref[slot, pl.ds(j, tc, stride=slab), :]
            lo = lax.bitcast_convert_type(lax.shift_left(w, 16), F32)
            hi = lax.bitcast_convert_type(w & jnp.int32(-65536), F32)
            xb_ref[:, j * LANES:(j + 1) * LANES] = lo.astype(BF16)
            xb_ref[:, half + j * LANES:half + (j + 1) * LANES] = hi.astype(BF16)
        acc_ref[...] = jnp.zeros_like(acc_ref)

    nxt = jnp.minimum(tile + 1, n_tiles - 1)
    for u in range(rows_per_step):
        row_copy(nxt, f * rows_per_step + u, 1 - slot).start()

    x = xb_ref[...]
    g = jnp.dot(x, wg_ref[...].astype(BF16), preferred_element_type=F32)
    up = jnp.dot(x, wu_ref[...].astype(BF16), preferred_element_type=F32)
    hmid = (g * _sigmoid(g) * up).astype(BF16)
    acc_ref[...] += jnp.dot(hmid, wd_ref[...].astype(BF16), preferred_element_type=F32)

    @pl.when(f == nf - 1)
    def _():
        o_ref[...] = acc_ref[...].astype(BF16)

    @pl.when((f == nf - 1) & (tile == n_tiles - 1))
    def _():
        wait_slot(1 - slot)


def _expert_ffn(idx_flat, h2p, w_gate, w_up, w_down, cap, tc, fc):
    n_exp, d, d_ff = w_gate.shape
    slab = d // 2 // LANES
    nct = cap // tc
    nf = d_ff // fc
    rows_per_step = -(-tc // nf)
    grid_spec = pltpu.PrefetchScalarGridSpec(
        num_scalar_prefetch=1,
        grid=(n_exp, nct, nf),
        in_specs=[pl.BlockSpec(memory_space=pl.ANY),
                  pl.BlockSpec((None, d, fc), lambda ei, ci, fi, idx: (ei, 0, fi)),
                  pl.BlockSpec((None, d, fc), lambda ei, ci, fi, idx: (ei, 0, fi)),
                  pl.BlockSpec((None, fc, d), lambda ei, ci, fi, idx: (ei, fi, 0))],
        out_specs=pl.BlockSpec((tc, d), lambda ei, ci, fi, idx: (ei * nct + ci, 0)),
        scratch_shapes=[pltpu.VMEM((2, rows_per_step * nf * slab, LANES), I32),
                        pltpu.VMEM((tc, d), BF16),
                        pltpu.VMEM((tc, d), F32),
                        pltpu.SemaphoreType.DMA((2,))],
    )
    return pl.pallas_call(
        functools.partial(_ffn_kernel, tc=tc, rows_per_step=rows_per_step, slab=slab),
        grid_spec=grid_spec,
        out_shape=jax.ShapeDtypeStruct((n_exp * cap, d), BF16),
        compiler_params=_cparams(("arbitrary", "arbitrary", "arbitrary")),
        name="expert_ffn",
    )(idx_flat, h2p, w_gate, w_up, w_down)


def _combine_kernel(tab_ref, aff_ref, tau_ref, need_ref, eqs_ref, x1_ref, g2_ref, y_hbm, o_ref,
                    ybuf_ref, acc_ref, sem, *, cap, n_tile):
    i = pl.program_id(0)
    tm, n_exp = aff_ref.shape
    slot_rows = COMBINE_SLOT
    k_rows = n_exp * slot_rows
    total_rows = n_exp * cap
    lane_e = lax.broadcasted_iota(I32, (1, n_exp), 1)

    def geometry(tile, e):
        n0 = tab_ref[e * (n_tile + 1) + tile]
        n1 = tab_ref[e * (n_tile + 1) + tile + 1]
        first = e * cap + n0
        aligned = (first // BF16_ROWS) * BF16_ROWS
        return aligned, first - aligned, n1 - n0

    def round_src(aligned, q):
        src = aligned + q * slot_rows
        clamped = jnp.minimum(src, total_rows - slot_rows)
        return clamped, src - clamped

    def issue_round(tile, q, buf):
        for e in range(n_exp):
            aligned, _, _ = geometry(tile, e)
            src, _ = round_src(aligned, q)
            pltpu.make_async_copy(y_hbm.at[pl.ds(pl.multiple_of(src, BF16_ROWS), slot_rows), :],
                                  ybuf_ref.at[buf, pl.ds(e * slot_rows, slot_rows), :], sem.at[buf]).start()

    buf = i % 2

    @pl.when(i == 0)
    def _():
        issue_round(i, 0, buf)

    @pl.when(i + 1 < n_tile)
    def _():
        issue_round(i + 1, 0, 1 - buf)

    a = aff_ref[...]
    tau = tau_ref[...]
    eq = a == tau
    ti = lax.broadcasted_iota(I32, (tm, tm), 0)
    tj = lax.broadcasted_iota(I32, (tm, tm), 1)
    eq_rank = eqs_ref[...] + jnp.dot(_ones_where(tj <= ti, BF16), _ones_where(eq, BF16),
                                     preferred_element_type=F32)
    sel = (a > tau) | (eq & (eq_rank <= need_ref[...]))
    wm = jnp.where(sel, a, 0.0).astype(BF16)
    rank = jnp.dot(_ones_where(tj < ti, BF16), _ones_where(sel, BF16), preferred_element_type=F32)

    n_round = jnp.int32(1)
    for e in range(n_exp):
        _, delta, n_sel = geometry(i, e)
        n_round = jnp.maximum(n_round, (delta + n_sel + slot_rows - 1) // slot_rows)

    spread = _ones_where(lax.broadcasted_iota(I32, (n_exp, k_rows), 1) // slot_rows
                         == lax.broadcasted_iota(I32, (n_exp, k_rows), 0), BF16)
    col_in_slot = (lax.broadcasted_iota(I32, (tm, k_rows), 1) % slot_rows).astype(F32)
    wm_cols = jnp.dot(wm, spread, preferred_element_type=F32)
    acc_ref[...] = jnp.zeros_like(acc_ref)

    def one_round(q, carry):
        @pl.when(q > 0)
        def _():
            issue_round(i, q, buf)

        pltpu.make_async_copy(ybuf_ref.at[buf], ybuf_ref.at[buf], sem.at[buf]).wait()
        offset = jnp.zeros((1, n_exp), F32)
        shift = jnp.zeros((1, n_exp), F32)
        for e in range(n_exp):
            aligned, delta, _ = geometry(i, e)
            _, sh = round_src(aligned, q)
            offset = jnp.where(lane_e == e, (delta - q * slot_rows).astype(F32), offset)
            shift = jnp.where(lane_e == e, sh.astype(F32), shift)
        u = rank + offset
        pos = jnp.where(u >= 0, u + shift, -1.0).astype(BF16)
        pos_cols = jnp.dot(pos, spread, preferred_element_type=F32)
        place = jnp.where(pos_cols == col_in_slot, wm_cols, 0.0).astype(BF16)
        acc_ref[...] += jnp.dot(place, ybuf_ref[buf], preferred_element_type=F32)
        return carry

    lax.fori_loop(0, n_round, one_round, 0)
    o_ref[...] = x1_ref[...] + g2_ref[...] * acc_ref[...]


def _combine(tab, aff, tau, need, eqs, x1, mod3, y, cap, seq_len, tm):
    t, d = x1.shape
    n_exp = aff.shape[1]
    n_tile = t // tm
    seq = lambda i, tab_: ((i * tm) // seq_len, 0, 5)
    grid_spec = pltpu.PrefetchScalarGridSpec(
        num_scalar_prefetch=1,
        grid=(n_tile,),
        in_specs=[pl.BlockSpec((tm, n_exp), lambda i, tab_: (i, 0)),
                  pl.BlockSpec((1, n_exp), lambda i, tab_: (0, 0)),
                  pl.BlockSpec((1, n_exp), lambda i, tab_: (0, 0)),
                  pl.BlockSpec((None, 1, n_exp), lambda i, tab_: (i, 0, 0)),
                  pl.BlockSpec((tm, d), lambda i, tab_: (i, 0)),
                  pl.BlockSpec((None, 1, d), seq),
                  pl.BlockSpec(memory_space=pl.ANY)],
        out_specs=pl.BlockSpec((tm, d), lambda i, tab_: (i, 0)),
        scratch_shapes=[pltpu.VMEM((2, n_exp * COMBINE_SLOT, d), BF16),
                        pltpu.VMEM((tm, d), F32),
                        pltpu.SemaphoreType.DMA((2,))],
    )
    return pl.pallas_call(
        functools.partial(_combine_kernel, cap=cap, n_tile=n_tile),
        grid_spec=grid_spec,
        out_shape=jax.ShapeDtypeStruct((t, d), F32),
        compiler_params=_cparams(("arbitrary",)),
        name="ec_combine",
    )(tab, aff, tau, need, eqs, x1, mod3, y)


def _moe(h2p, aff, afft, x1, mod_g, w_gate, w_up, w_down, seq_len):
    t = aff.shape[0]
    n_exp = aff.shape[1]
    cap = CAPACITY_FACTOR * t // n_exp
    tm = COMBINE_TILE
    idx, tau, need, pref, eqpref = _route(afft.reshape(n_exp, t // LANES, LANES), cap)
    y = _expert_ffn(idx.reshape(-1), h2p, w_gate, w_up, w_down, cap, min(1024, cap), 256)
    rows_per_tile = tm // LANES
    tab = jnp.concatenate([pref[:, 0, ::rows_per_tile], jnp.full((n_exp, 1), cap, I32)], axis=1).reshape(-1)
    eqs = eqpref[:, 0, ::rows_per_tile].T.reshape(t // tm, 1, n_exp)
    return _combine(tab, aff, tau[:, 0, 0].reshape(1, n_exp), need[:, 0, 0].reshape(1, n_exp), eqs,
                    x1, mod_g, y, cap, seq_len, tm)


def _trunk(x, mod3, seq_base, prm):
    b, s, d = x.shape
    t = b * s
    x2 = x.reshape(t, d)
    mod_g = lax.slice_in_dim(mod3, seq_base, seq_base + b, axis=0)
    z = _in_projection(x2, mod_g, prm["norm1_g"], prm["w_in"], prm["q_g"], prm["k_g"], s, min(1024, s))
    attn_o = _attention(z, prm["sink"], s)
    x1, h2p, aff, afft = _post_mixer(attn_o, z, x2, mod_g, prm["norm2_g"], prm["ln_g"], prm["ln_b"], prm["ws"],
                                     prm["bs"], prm["w_oa"], prm["w_os"], prm["w_out"], prm["w_r2"], s, 256)
    out = _moe(h2p, aff, afft, x1, mod_g, prm["w_gate"], prm["w_up"], prm["w_down"], s)
    return out.reshape(b, s, d)


def kernel(x_prompt, x_sample, c_prompt, c_sample, w_ada, b_ada, norm1_g, norm2_g, w_in, q_norm_g, k_norm_g,
           attn_sink, sgu_ln_g, sgu_ln_b, sgu_w, sgu_b, w_o_attn, w_o_sgu, w_out, w_router, w_gate, w_up,
           w_down):
    assert w_ada.shape[0] == 1
    bp, bs_ = x_prompt.shape[0], x_sample.shape[0]
    assert bp + bs_ <= N_SEQ_PAD
    d = D_MODEL
    c_pad = jnp.zeros((N_SEQ_PAD, d), F32).at[:bp].set(c_prompt).at[bp:bp + bs_].set(c_sample)
    mod = _modulation(c_pad, w_ada[0], b_ada[0])
    mod3 = mod.reshape(N_SEQ_PAD, 1, 6 * d)

    w = w_in[0]
    q_end = N_Q_HEADS * HEAD_DIM
    kv_end = q_end + 2 * KV_WIDTH
    w_in_r = jnp.concatenate([w[:, :q_end], w[:, kv_end:], w[:, q_end:kv_end]], axis=1).astype(BF16)
    w_r = w_router[0]
    w_r_hi = w_r.astype(BF16)
    w_r_lo = (w_r - w_r_hi.astype(F32)).astype(BF16)
    pad = ((0, 0), (0, LANES - N_EXPERTS))
    w_r2 = jnp.concatenate([jnp.pad(w_r_hi, pad), jnp.pad(w_r_lo, pad)], axis=1)
    prm = dict(
        norm1_g=norm1_g[0].reshape(1, d), norm2_g=norm2_g[0].reshape(1, d), w_in=w_in_r,
        q_g=q_norm_g[0].reshape(1, HEAD_DIM), k_g=k_norm_g[0].reshape(1, HEAD_DIM), sink=attn_sink[0],
        ln_g=sgu_ln_g[0].reshape(1, d), ln_b=sgu_ln_b[0].reshape(1, d),
        ws=sgu_w[0].astype(BF16),
        bs=jnp.broadcast_to(sgu_b[0][:, :, None], (N_SGU_GROUPS, BLOCK, BLOCK)),
        w_oa=w_o_attn[0].astype(BF16), w_os=w_o_sgu[0].astype(BF16), w_out=w_out[0].astype(BF16),
        w_r2=w_r2, w_gate=w_gate[0], w_up=w_up[0], w_down=w_down[0])
    y_prompt = _trunk(x_prompt, mod3, 0, prm)
    y_sample = _trunk(x_sample, mod3, bp, prm)
    return (y_prompt, y_sample)
```

```python
import functools

import jax
import jax.numpy as jnp
from jax import lax
from jax.experimental import pallas as pl
from jax.experimental.pallas import tpu as pltpu

F32 = jnp.float32
BF16 = jnp.bfloat16
I32 = jnp.int32

D_MODEL = 2048
HEAD_DIM = 128
N_Q_HEADS = 16
N_KV_HEADS = 4
Q_PER_KV = N_Q_HEADS // N_KV_HEADS
KV_WIDTH = N_KV_HEADS * HEAD_DIM
BLOCK = 128
N_SGU_GROUPS = 16
N_EXPERTS = 16
CAPACITY_FACTOR = 2
D_FF = 2816
NORM_EPS = 1e-6
IN_WIDTH = 11264
N_SEQ_PAD = 8

LANES = 128
SUBLANES = 8
BF16_ROWS = 16
MXU_DIM = 256

COL_TILE = 1024
Q_TILES = (0, 2)
GELU_TILES = (2, 6)
SIG_TILES = (6, 10)
KV_TILE = 10
N_COL_TILES = IN_WIDTH // COL_TILE

INPROJ_ROW_CHUNK = 256
ATTN_Q_BLOCKS = 2
ROUTE_CHUNK_GROUP = 4
COMBINE_TILE = 256
COMBINE_SLOT = 64

VMEM_LIMIT = 56 * 1024 * 1024


def _cparams(sem):
    return pltpu.CompilerParams(dimension_semantics=sem, vmem_limit_bytes=VMEM_LIMIT)


def _gelu_tanh(x):
    c = 0.7978845608028654
    return 0.5 * x * (1.0 + jnp.tanh(c * (x + 0.044715 * (x * x * x))))


def _sigmoid(x):
    return 1.0 / (1.0 + jnp.exp(-x))


def _ones_where(cond, dtype=F32):
    return jnp.where(cond, 1.0, 0.0).astype(dtype)


def _mod_kernel(c_ref, w_ref, b_ref, o_ref):
    c = c_ref[...]
    s = c * _sigmoid(c)
    o_ref[...] = jnp.dot(s, w_ref[...], precision=lax.Precision.HIGHEST,
                         preferred_element_type=F32) + b_ref[...]


def _modulation(c_pad, w_ada, b_ada):
    n = w_ada.shape[1]
    tn = 1024
    return pl.pallas_call(
        _mod_kernel,
        grid=(n // tn,),
        in_specs=[pl.BlockSpec((N_SEQ_PAD, D_MODEL), lambda j: (0, 0)),
                  pl.BlockSpec((D_MODEL, tn), lambda j: (0, j)),
                  pl.BlockSpec((1, tn), lambda j: (0, j))],
        out_specs=pl.BlockSpec((N_SEQ_PAD, tn), lambda j: (0, j)),
        out_shape=jax.ShapeDtypeStruct((N_SEQ_PAD, n), F32),
        compiler_params=_cparams(("arbitrary",)),
        name="adaln_mod",
    )(c_pad, w_ada, b_ada.reshape(1, n))


def _inproj_kernel(x_ref, sh_ref, sc_ref, g_ref, w_ref, qg_ref, kg_ref, o_ref, h_ref):
    j = pl.program_id(1)

    @pl.when(j == 0)
    def _():
        x = x_ref[...]
        ms = jnp.mean(x * x, axis=-1, keepdims=True)
        y = x * lax.rsqrt(ms + NORM_EPS) * g_ref[...]
        h_ref[...] = (y * (1.0 + sc_ref[...]) + sh_ref[...]).astype(BF16)

    def head_norm(sub, g):
        ms = jnp.mean(sub * sub, axis=-1, keepdims=True)
        return sub * lax.rsqrt(ms + NORM_EPS) * g

    def chunked(epilogue):
        for r in range(x_ref.shape[0] // INPROJ_ROW_CHUNK):
            rows = pl.ds(r * INPROJ_ROW_CHUNK, INPROJ_ROW_CHUNK)
            epilogue(rows, jnp.dot(h_ref[rows, :], w_ref[...], preferred_element_type=F32))

    @pl.when(j < Q_TILES[1])
    def _():
        g = qg_ref[...] * (HEAD_DIM ** -0.5)

        def epilogue(rows, acc):
            for hh in range(COL_TILE // HEAD_DIM):
                cols = slice(hh * HEAD_DIM, (hh + 1) * HEAD_DIM)
                o_ref[rows, cols] = head_norm(acc[:, cols], g).astype(BF16)
        chunked(epilogue)

    @pl.when((j >= GELU_TILES[0]) & (j < GELU_TILES[1]))
    def _():
        def epilogue(rows, acc):
            o_ref[rows, :] = _gelu_tanh(acc).astype(BF16)
        chunked(epilogue)

    @pl.when((j >= SIG_TILES[0]) & (j < SIG_TILES[1]))
    def _():
        def epilogue(rows, acc):
            o_ref[rows, :] = _sigmoid(acc).astype(BF16)
        chunked(epilogue)

    @pl.when(j == KV_TILE)
    def _():
        g = kg_ref[...]

        def epilogue(rows, acc):
            for hh in range(N_KV_HEADS):
                cols = slice(hh * HEAD_DIM, (hh + 1) * HEAD_DIM)
                o_ref[rows, cols] = head_norm(acc[:, cols], g).astype(BF16)
            o_ref[rows, KV_WIDTH:] = acc[:, KV_WIDTH:].astype(BF16)
        chunked(epilogue)


def _in_projection(x2, mod3, norm_g, w_in_b, q_g, k_g, seq_len, tm):
    t = x2.shape[0]
    seq = lambda i: (i * tm) // seq_len
    w_tile = lambda j: jnp.where(j < Q_TILES[1], j, jnp.where(j == KV_TILE, Q_TILES[1], j + 1))
    return pl.pallas_call(
        _inproj_kernel,
        grid=(t // tm, N_COL_TILES),
        in_specs=[pl.BlockSpec((tm, D_MODEL), lambda i, j: (i, 0)),
                  pl.BlockSpec((None, 1, D_MODEL), lambda i, j: (seq(i), 0, 0)),
                  pl.BlockSpec((None, 1, D_MODEL), lambda i, j: (seq(i), 0, 1)),
                  pl.BlockSpec((1, D_MODEL), lambda i, j: (0, 0)),
                  pl.BlockSpec((D_MODEL, COL_TILE), lambda i, j: (0, w_tile(j))),
                  pl.BlockSpec((1, HEAD_DIM), lambda i, j: (0, 0)),
                  pl.BlockSpec((1, HEAD_DIM), lambda i, j: (0, 0))],
        out_specs=pl.BlockSpec((tm, COL_TILE), lambda i, j: (i, j)),
        out_shape=jax.ShapeDtypeStruct((t, IN_WIDTH), BF16),
        scratch_shapes=[pltpu.VMEM((tm, D_MODEL), BF16)],
        compiler_params=_cparams(("arbitrary", "arbitrary")),
        name="norm_inproj",
    )(x2, mod3, mod3, norm_g, w_in_b, q_g, k_g)


def _alibi_slope(head):
    return 2.0 ** (-8.0 * (head + 1) / N_Q_HEADS)


def _attn_kernel(sink_ref, q_ref, kp_ref, kc_ref, kn_ref, vp_ref, vc_ref, vn_ref, o_ref, *, steps_per_seq):
    n = pl.program_id(0)
    pos = n % steps_per_seq
    first = pos == 0
    last = pos == steps_per_seq - 1

    qi = lax.broadcasted_iota(I32, (BLOCK, 3 * BLOCK), 0)
    kj = lax.broadcasted_iota(I32, (BLOCK, 3 * BLOCK), 1)
    absrel = jnp.abs(kj - BLOCK - qi)
    in_window = absrel <= BLOCK
    absrel_f = absrel.astype(F32)
    pos_inf = jnp.float32(jnp.inf)

    for b in range(ATTN_Q_BLOCKS):
        q_rows = slice(b * BLOCK, (b + 1) * BLOCK)
        valid = in_window
        if b == 0:
            valid = valid & jnp.logical_not(first & (kj < BLOCK))
        if b == ATTN_Q_BLOCKS - 1:
            valid = valid & jnp.logical_not(last & (kj >= 2 * BLOCK))
        for h in range(N_KV_HEADS):
            kv_cols = slice(h * HEAD_DIM, (h + 1) * HEAD_DIM)
            k_parts = [kp_ref[:, kv_cols]] + [kc_ref[c * BLOCK:(c + 1) * BLOCK, kv_cols]
                                              for c in range(ATTN_Q_BLOCKS)] + [kn_ref[:, kv_cols]]
            v_parts = [vp_ref[:, kv_cols]] + [vc_ref[c * BLOCK:(c + 1) * BLOCK, kv_cols]
                                              for c in range(ATTN_Q_BLOCKS)] + [vn_ref[:, kv_cols]]
            kcat = jnp.concatenate(k_parts[b:b + 3], axis=0)
            vcat = jnp.concatenate(v_parts[b:b + 3], axis=0)
            heads = [h * Q_PER_KV + g for g in range(Q_PER_KV)]
            cols = [slice(hd * HEAD_DIM, (hd + 1) * HEAD_DIM) for hd in heads]
            sinks = [sink_ref[hd] for hd in heads]
            s = [lax.dot_general(q_ref[q_rows, c], kcat, (((1,), (1,)), ((), ())), preferred_element_type=F32)
                 for c in cols]
            s = [sg - jnp.where(valid, _alibi_slope(hd) * absrel_f, pos_inf) for sg, hd in zip(s, heads)]
            m = [jnp.maximum(jnp.max(sg, axis=-1, keepdims=True), sk) for sg, sk in zip(s, sinks)]
            p = [jnp.exp(sg - mg) for sg, mg in zip(s, m)]
            denom = [jnp.sum(pg, axis=-1, keepdims=True) + jnp.exp(sk - mg) for pg, sk, mg in zip(p, sinks, m)]
            o = [jnp.dot(pg.astype(BF16), vcat, preferred_element_type=F32) for pg in p]
            for c, og, dg in zip(cols, o, denom):
                o_ref[q_rows, c] = (og / dg).astype(BF16)


def _attention(z, sink, seq_len):
    t = z.shape[0]
    nb = t // BLOCK
    qb = ATTN_Q_BLOCKS
    k_col = (IN_WIDTH - 2 * KV_WIDTH) // KV_WIDTH
    v_col = k_col + 1
    halo = lambda col: [pl.BlockSpec((BLOCK, KV_WIDTH), lambda n, s: (jnp.maximum(qb * n - 1, 0), col)),
                        pl.BlockSpec((qb * BLOCK, KV_WIDTH), lambda n, s: (n, col)),
                        pl.BlockSpec((BLOCK, KV_WIDTH), lambda n, s: (jnp.minimum(qb * n + qb, nb - 1), col))]
    grid_spec = pltpu.PrefetchScalarGridSpec(
        num_scalar_prefetch=1,
        grid=(nb // qb,),
        in_specs=[pl.BlockSpec((qb * BLOCK, D_MODEL), lambda n, s: (n, 0))] + halo(k_col) + halo(v_col),
        out_specs=pl.BlockSpec((qb * BLOCK, D_MODEL), lambda n, s: (n, 0)),
    )
    return pl.pallas_call(
        functools.partial(_attn_kernel, steps_per_seq=seq_len // (qb * BLOCK)),
        grid_spec=grid_spec,
        out_shape=jax.ShapeDtypeStruct((t, D_MODEL), BF16),
        compiler_params=_cparams(("arbitrary",)),
        name="window_attn",
    )(sink, z, z, z, z, z, z, z)


def _post_kernel(attn_ref, u_ref, vg_ref, ga_ref, gs_ref, x_ref, g1_ref, sh2_ref, sc2_ref, n2g_ref,
                 lng_ref, lnb_ref, ws_ref, bs_ref, woa_ref, wos_ref, wout_ref, wr_ref,
                 x1_ref, h2p_ref, aff_ref, afft_ref, sgu_ref):
    tm = x_ref.shape[0]
    vg = vg_ref[...].astype(F32)
    mu = jnp.mean(vg, axis=-1, keepdims=True)
    cen = vg - mu
    var = jnp.mean(cen * cen, axis=-1, keepdims=True)
    vn = (cen * lax.rsqrt(var + NORM_EPS) * lng_ref[...] + lnb_ref[...]).astype(BF16)
    for c in range(tm // BLOCK):
        rows = slice(c * BLOCK, (c + 1) * BLOCK)
        for g in range(N_SGU_GROUPS):
            cols = slice(g * BLOCK, (g + 1) * BLOCK)
            mixed = jnp.dot(ws_ref[g], vn[rows, cols], preferred_element_type=F32) + bs_ref[g]
            sgu_ref[rows, cols] = (u_ref[rows, cols].astype(F32) * mixed).astype(BF16)
    a = jnp.dot(attn_ref[...], woa_ref[...], preferred_element_type=F32)
    s = jnp.dot(sgu_ref[...], wos_ref[...], preferred_element_type=F32)
    merged = (ga_ref[...].astype(F32) * a + gs_ref[...].astype(F32) * s).astype(BF16)
    mix = jnp.dot(merged, wout_ref[...], preferred_element_type=F32)
    x1 = x_ref[...] + g1_ref[...] * mix
    x1_ref[...] = x1
    ms = jnp.mean(x1 * x1, axis=-1, keepdims=True)
    h2 = x1 * lax.rsqrt(ms + NORM_EPS) * n2g_ref[...]
    h2 = h2 * (1.0 + sc2_ref[...]) + sh2_ref[...]
    h2b = h2.astype(BF16)
    half = D_MODEL // 2
    n_slab = half // LANES
    lo = lax.shift_right_logical(lax.bitcast_convert_type(h2b[:, :half].astype(F32), I32), 16)
    hi = lax.bitcast_convert_type(h2b[:, half:].astype(F32), I32) & jnp.int32(-65536)
    word = hi | lo
    for j in range(n_slab):
        h2p_ref[pl.ds(j, tm, stride=n_slab), :] = word[:, j * LANES:(j + 1) * LANES]
    h_lo = (h2 - h2b.astype(F32)).astype(BF16)
    r1 = jnp.dot(h2b, wr_ref[...], preferred_element_type=F32)
    r2 = jnp.dot(h_lo, wr_ref[:, :LANES], preferred_element_type=F32)
    logits = r1[:, :LANES] + r1[:, LANES:] + r2
    lane = lax.broadcasted_iota(I32, logits.shape, 1)
    logits = jnp.where(lane < N_EXPERTS, logits, -jnp.inf)
    logits = logits - jnp.max(logits, axis=-1, keepdims=True)
    ex = jnp.exp(logits)
    aff = ex / jnp.sum(ex, axis=-1, keepdims=True)
    aff_ref[...] = aff[:, :N_EXPERTS]
    afft_ref[...] = aff.T[:N_EXPERTS, :]


def _post_mixer(attn_o, z, x2, mod3, norm2_g, ln_g, ln_b, ws, bs, w_oa, w_os, w_out, w_r2, seq_len, tm):
    t = x2.shape[0]
    n_slab = D_MODEL // 2 // LANES
    seq = lambda i: (i * tm) // seq_len
    const2 = lambda i: (0, 0)
    const3 = lambda i: (0, 0, 0)
    resident = lambda shape, imap: pl.BlockSpec(shape, imap, pipeline_mode=pl.Buffered(1))
    tok = lambda col: pl.BlockSpec((tm, D_MODEL), lambda i: (i, col))
    modv = lambda col: pl.BlockSpec((None, 1, D_MODEL), lambda i: (seq(i), 0, col))
    return pl.pallas_call(
        _post_kernel,
        grid=(t // tm,),
        in_specs=[tok(0),
                  tok(1), tok(2), tok(3), tok(4),
                  tok(0),
                  modv(2), modv(3), modv(4),
                  pl.BlockSpec((1, D_MODEL), const2),
                  pl.BlockSpec((1, D_MODEL), const2),
                  pl.BlockSpec((1, D_MODEL), const2),
                  resident((N_SGU_GROUPS, BLOCK, BLOCK), const3),
                  resident((N_SGU_GROUPS, BLOCK, BLOCK), const3),
                  resident((D_MODEL, D_MODEL), const2),
                  resident((D_MODEL, D_MODEL), const2),
                  resident((D_MODEL, D_MODEL), const2),
                  resident((D_MODEL, 2 * LANES), const2)],
        out_specs=[pl.BlockSpec((tm, D_MODEL), lambda i: (i, 0)),
                   pl.BlockSpec((tm * n_slab, LANES), lambda i: (i, 0)),
                   pl.BlockSpec((tm, N_EXPERTS), lambda i: (i, 0)),
                   pl.BlockSpec((N_EXPERTS, tm), lambda i: (0, i))],
        out_shape=[jax.ShapeDtypeStruct((t, D_MODEL), F32),
                   jax.ShapeDtypeStruct((t * n_slab, LANES), I32),
                   jax.ShapeDtypeStruct((t, N_EXPERTS), F32),
                   jax.ShapeDtypeStruct((N_EXPERTS, t), F32)],
        scratch_shapes=[pltpu.VMEM((tm, D_MODEL), BF16)],
        compiler_params=_cparams(("arbitrary",)),
        name="post_mixer",
    )(attn_o, z, z, z, z, x2, mod3, mod3, mod3, norm2_g, ln_g, ln_b, ws, bs, w_oa, w_os, w_out, w_r2)


def _route_kernel(aff_ref, idx_ref, tau_ref, need_ref, pref_ref, eqpref_ref, bits_ref, taus_ref, *, cap):
    n_exp, n_rows, _ = aff_ref.shape
    n_tok = n_rows * LANES
    bits_ref[...] = lax.bitcast_convert_type(aff_ref[...], I32)

    def bisect(i, v):
        cand = v | lax.shift_left(jnp.int32(1), 30 - i)
        ge = _ones_where(bits_ref[...] >= cand)
        cnt = jnp.sum(jnp.sum(ge, axis=1, keepdims=True), axis=2, keepdims=True)
        return jnp.where(cnt >= cap, cand, v)

    taus_ref[...] = lax.fori_loop(0, 31, bisect, jnp.zeros((n_exp, 1, LANES), I32))

    li = lax.broadcasted_iota(I32, (LANES, LANES), 0)
    lj = lax.broadcasted_iota(I32, (LANES, LANES), 1)
    upper_incl = _ones_where(li <= lj, BF16)
    ones_sq = jnp.ones((LANES, LANES), BF16)
    ri = lax.broadcasted_iota(I32, (n_rows, n_rows), 0)
    rj = lax.broadcasted_iota(I32, (n_rows, n_rows), 1)
    lower_strict = _ones_where(rj < ri, BF16)
    upper_strict = _ones_where(ri < rj, BF16)
    ones_rows = jnp.ones((SUBLANES, LANES), BF16)
    nt_dims = (((1,), (1,)), ((), ()))

    def incl_cumsum(mb):
        local = jnp.dot(mb, upper_incl, preferred_element_type=F32)
        totb = jnp.dot(mb, ones_sq, preferred_element_type=F32)
        prefc = jnp.dot(lower_strict, totb.astype(BF16), preferred_element_type=F32)
        return local + prefc

    def row_prefix(mb):
        tot_row = lax.dot_general(ones_rows, mb, nt_dims, preferred_element_type=F32)
        pref_row = jnp.dot(tot_row.astype(BF16), upper_strict, preferred_element_type=F32)
        return tot_row, pref_row

    def per_expert(e, carry):
        t = taus_ref[e]
        b = bits_ref[e]
        gt = b > t
        eq = b == t
        eqb = _ones_where(eq, BF16)
        need = cap - jnp.sum(_ones_where(gt))
        sel = gt | (eq & (incl_cumsum(eqb) <= need))
        m = _ones_where(sel, BF16)
        glob = incl_cumsum(m)
        tot_row, pref_row = row_prefix(m)
        incl_row = pref_row + tot_row
        _, eq_pref_row = row_prefix(eqb)
        pref_ref[e] = pref_row.astype(I32)
        eqpref_ref[e] = eq_pref_row
        tau_ref[e] = jnp.broadcast_to(lax.bitcast_convert_type(t, F32), (SUBLANES, LANES))
        need_ref[e] = jnp.full((SUBLANES, LANES), need, F32)
        ghi = jnp.floor(glob * (1.0 / MXU_DIM))
        glo = (glob - MXU_DIM * ghi).astype(BF16)
        ghi = ghi.astype(BF16)
        pr = pref_row[0:1, :]
        ir = incl_row[0:1, :]

        n_chunk = cap // LANES
        group = ROUTE_CHUNK_GROUP if n_chunk % ROUTE_CHUNK_GROUP == 0 else 1

        def chunks(cg, carry2):
            cs = [cg * group + k for k in range(group)]
            s_r = [(c * LANES + lax.broadcasted_iota(I32, (LANES, n_rows), 0)).astype(F32) for c in cs]
            onehot = [_ones_where((pr <= s) & (s < ir), BF16) for s in s_r]
            rowid = [jnp.sum(_ones_where(ir <= s), axis=-1, keepdims=True) for s in s_r]
            grow = [MXU_DIM * jnp.dot(oh, ghi, preferred_element_type=F32)
                    + jnp.dot(oh, glo, preferred_element_type=F32) for oh in onehot]
            s_l = [(c * LANES + lax.broadcasted_iota(I32, (LANES, LANES), 0)).astype(F32) for c in cs]
            inrow = [jnp.sum(_ones_where(gr <= s), axis=-1, keepdims=True) for gr, s in zip(grow, s_l)]
            tok = [jnp.minimum(r * LANES + q, n_tok - 1.0) for r, q in zip(rowid, inrow)]
            tok_t = [jnp.broadcast_to(t_, (LANES, LANES)).T for t_ in tok]
            for c, tt in zip(cs, tok_t):
                idx_ref[e, pl.ds(c, 1), :] = tt[0:1, :].astype(I32)
            return carry2

        lax.fori_loop(0, n_chunk // group, chunks, 0)
        return carry

    lax.fori_loop(0, n_exp, per_expert, 0)


def _route(aff3, cap):
    n_exp, n_rows, _ = aff3.shape
    rep = lambda dt, w: jax.ShapeDtypeStruct((n_exp, SUBLANES, w), dt)
    return pl.pallas_call(
        functools.partial(_route_kernel, cap=cap),
        out_shape=[jax.ShapeDtypeStruct((n_exp, cap // LANES, LANES), I32),
                   rep(F32, LANES), rep(F32, LANES), rep(I32, n_rows), rep(F32, n_rows)],
        scratch_shapes=[pltpu.VMEM((n_exp, n_rows, LANES), I32), pltpu.VMEM((n_exp, 1, LANES), I32)],
        compiler_params=pltpu.CompilerParams(vmem_limit_bytes=VMEM_LIMIT),
        name="ec_route",
    )(aff3)


def _ffn_kernel(idx_ref, h2p_hbm, wg_hbm, wu_hbm, wd_hbm, o_ref, xraw_ref, xb_ref, acc_ref, wgb_ref, wub_ref,
                wdb_ref, sem, wsem, *, tc, rows_per_step, slab, fc, nf):
    ct_n = pl.num_programs(1)
    expert = pl.program_id(0)
    tile = expert * ct_n + pl.program_id(1)
    n_tiles = pl.num_programs(0) * ct_n
    slot = tile % 2
    nxt = jnp.minimum(tile + 1, n_tiles - 1)
    w_parity = (tile * nf) % 2

    def row_copy(tile_id, s, dst_slot):
        tok = idx_ref[tile_id * tc + jnp.minimum(s, tc - 1)]
        return pltpu.make_async_copy(
            h2p_hbm.at[pl.ds(pl.multiple_of(tok * slab, slab), slab), :],
            xraw_ref.at[dst_slot, pl.ds(pl.multiple_of(s * slab, slab), slab), :], sem.at[dst_slot])

    def wait_slot(s_):
        pltpu.make_async_copy(xraw_ref.at[s_], xraw_ref.at[s_], sem.at[s_]).wait()

    def weight_copies(e, f, ws):
        cols = pl.ds(pl.multiple_of(f * fc, fc), fc)
        return (pltpu.make_async_copy(wg_hbm.at[e, :, cols], wgb_ref.at[ws], wsem.at[ws]),
                pltpu.make_async_copy(wu_hbm.at[e, :, cols], wub_ref.at[ws], wsem.at[ws]),
                pltpu.make_async_copy(wd_hbm.at[e, cols, :], wdb_ref.at[ws], wsem.at[ws]))

    @pl.when(tile == 0)
    def _():
        def body(s, carry):
            row_copy(tile, s, slot).start()
            return carry
        lax.fori_loop(0, rows_per_step * nf, body, 0)
        for c in weight_copies(expert, 0, w_parity):
            c.start()

    wait_slot(slot)
    half = slab * LANES
    for j in range(slab):
        w = xraw_ref[slot, pl.ds(j, tc, stride=slab), :]
        lo = lax.bitcast_convert_type(lax.shift_left(w, 16), F32)
        hi = lax.bitcast_convert_type(w & jnp.int32(-65536), F32)
        xb_ref[:, j * LANES:(j + 1) * LANES] = lo.astype(BF16)
        xb_ref[:, half + j * LANES:half + (j + 1) * LANES] = hi.astype(BF16)
    acc_ref[...] = jnp.zeros_like(acc_ref)

    def chunk(f, carry):
        ws = (w_parity + f) % 2
        for c in weight_copies(expert, f, ws):
            c.wait()
        last = f == nf - 1

        @pl.when(jnp.logical_not(last & (tile == n_tiles - 1)))
        def _():
            for c in weight_copies(jnp.where(last, nxt // ct_n, expert), jnp.where(last, 0, f + 1), 1 - ws):
                c.start()

        for u in range(rows_per_step):
            row_copy(nxt, f * rows_per_step + u, 1 - slot).start()

        x = xb_ref[...]
        g = jnp.dot(x, wgb_ref[ws].astype(BF16), preferred_element_type=F32)
        up = jnp.dot(x, wub_ref[ws].astype(BF16), preferred_element_type=F32)
        hmid = (g * _sigmoid(g) * up).astype(BF16)
        acc_ref[...] += jnp.dot(hmid, wdb_ref[ws].astype(BF16), preferred_element_type=F32)
        return carry

    lax.fori_loop(0, nf, chunk, 0)
    o_ref[...] = acc_ref[...].astype(BF16)

    @pl.when(tile == n_tiles - 1)
    def _():
        wait_slot(1 - slot)


def _expert_ffn(idx_flat, h2p, w_gate, w_up, w_down, cap, tc, fc):
    n_exp, d, d_ff = w_gate.shape
    slab = d // 2 // LANES
    nct = cap // tc
    nf = d_ff // fc
    rows_per_step = -(-tc // nf)
    any_spec = pl.BlockSpec(memory_space=pl.ANY)
    grid_spec = pltpu.PrefetchScalarGridSpec(
        num_scalar_prefetch=1,
        grid=(n_exp, nct),
        in_specs=[any_spec, any_spec, any_spec, any_spec],
        out_specs=pl.BlockSpec((tc, d), lambda ei, ci, idx: (ei * nct + ci, 0)),
        scratch_shapes=[pltpu.VMEM((2, rows_per_step * nf * slab, LANES), I32),
                        pltpu.VMEM((tc, d), BF16),
                        pltpu.VMEM((tc, d), F32),
                        pltpu.VMEM((2, d, fc), w_gate.dtype),
                        pltpu.VMEM((2, d, fc), w_up.dtype),
                        pltpu.VMEM((2, fc, d), w_down.dtype),
                        pltpu.SemaphoreType.DMA((2,)),
                        pltpu.SemaphoreType.DMA((2,))],
    )
    return pl.pallas_call(
        functools.partial(_ffn_kernel, tc=tc, rows_per_step=rows_per_step, slab=slab, fc=fc, nf=nf),
        grid_spec=grid_spec,
        out_shape=jax.ShapeDtypeStruct((n_exp * cap, d), BF16),
        compiler_params=_cparams(("arbitrary", "arbitrary")),
        name="expert_ffn",
    )(idx_flat, h2p, w_gate, w_up, w_down)


def _combine_kernel(tab_ref, aff_ref, tau_ref, need_ref, eqs_ref, x1_ref, g2_ref, y_hbm, o_ref,
                    ybuf_ref, acc_ref, sem, *, cap, n_tile):
    i = pl.program_id(0)
    tm, n_exp = aff_ref.shape
    slot_rows = COMBINE_SLOT
    k_rows = n_exp * slot_rows
    total_rows = n_exp * cap
    lane_e = lax.broadcasted_iota(I32, (1, n_exp), 1)

    def geometry(tile, e):
        n0 = tab_ref[e * (n_tile + 1) + tile]
        n1 = tab_ref[e * (n_tile + 1) + tile + 1]
        first = e * cap + n0
        aligned = (first // BF16_ROWS) * BF16_ROWS
        return aligned, first - aligned, n1 - n0

    def round_src(aligned, q):
        src = aligned + q * slot_rows
        clamped = jnp.minimum(src, total_rows - slot_rows)
        return clamped, src - clamped

    def issue_round(tile, q, buf):
        for e in range(n_exp):
            aligned, _, _ = geometry(tile, e)
            src, _ = round_src(aligned, q)
            pltpu.make_async_copy(y_hbm.at[pl.ds(pl.multiple_of(src, BF16_ROWS), slot_rows), :],
                                  ybuf_ref.at[buf, pl.ds(e * slot_rows, slot_rows), :], sem.at[buf]).start()

    buf = i % 2

    @pl.when(i == 0)
    def _():
        issue_round(i, 0, buf)

    @pl.when(i + 1 < n_tile)
    def _():
        issue_round(i + 1, 0, 1 - buf)

    a = aff_ref[...]
    tau = tau_ref[...]
    eq = a == tau
    ti = lax.broadcasted_iota(I32, (tm, tm), 0)
    tj = lax.broadcasted_iota(I32, (tm, tm), 1)
    eq_rank = eqs_ref[...] + jnp.dot(_ones_where(tj <= ti, BF16), _ones_where(eq, BF16),
                                     preferred_element_type=F32)
    sel = (a > tau) | (eq & (eq_rank <= need_ref[...]))
    wm = jnp.where(sel, a, 0.0).astype(BF16)
    rank = jnp.dot(_ones_where(tj < ti, BF16), _ones_where(sel, BF16), preferred_element_type=F32)

    n_round = jnp.int32(1)
    for e in range(n_exp):
        _, delta, n_sel = geometry(i, e)
        n_round = jnp.maximum(n_round, (delta + n_sel + slot_rows - 1) // slot_rows)

    spread = _ones_where(lax.broadcasted_iota(I32, (n_exp, k_rows), 1) // slot_rows
                         == lax.broadcasted_iota(I32, (n_exp, k_rows), 0), BF16)
    col_in_slot = (lax.broadcasted_iota(I32, (tm, k_rows), 1) % slot_rows).astype(F32)
    wm_cols = jnp.dot(wm, spread, preferred_element_type=F32)
    acc_ref[...] = jnp.zeros_like(acc_ref)

    def one_round(q, carry):
        @pl.when(q > 0)
        def _():
            issue_round(i, q, buf)

        pltpu.make_async_copy(ybuf_ref.at[buf], ybuf_ref.at[buf], sem.at[buf]).wait()
        offset = jnp.zeros((1, n_exp), F32)
        shift = jnp.zeros((1, n_exp), F32)
        for e in range(n_exp):
            aligned, delta, _ = geometry(i, e)
            _, sh = round_src(aligned, q)
            offset = jnp.where(lane_e == e, (delta - q * slot_rows).astype(F32), offset)
            shift = jnp.where(lane_e == e, sh.astype(F32), shift)
        u = rank + offset
        pos = jnp.where(u >= 0, u + shift, -1.0).astype(BF16)
        pos_cols = jnp.dot(pos, spread, preferred_element_type=F32)
        place = jnp.where(pos_cols == col_in_slot, wm_cols, 0.0).astype(BF16)
        acc_ref[...] += jnp.dot(place, ybuf_ref[buf], preferred_element_type=F32)
        return carry

    lax.fori_loop(0, n_round, one_round, 0)
    o_ref[...] = x1_ref[...] + g2_ref[...] * acc_ref[...]


def _combine(tab, aff, tau, need, eqs, x1, mod3, y, cap, seq_len, tm):
    t, d = x1.shape
    n_exp = aff.shape[1]
    n_tile = t // tm
    seq = lambda i, tab_: ((i * tm) // seq_len, 0, 5)
    grid_spec = pltpu.PrefetchScalarGridSpec(
        num_scalar_prefetch=1,
        grid=(n_tile,),
        in_specs=[pl.BlockSpec((tm, n_exp), lambda i, tab_: (i, 0)),
                  pl.BlockSpec((1, n_exp), lambda i, tab_: (0, 0)),
                  pl.BlockSpec((1, n_exp), lambda i, tab_: (0, 0)),
                  pl.BlockSpec((None, 1, n_exp), lambda i, tab_: (i, 0, 0)),
                  pl.BlockSpec((tm, d), lambda i, tab_: (i, 0)),
                  pl.BlockSpec((None, 1, d), seq),
                  pl.BlockSpec(memory_space=pl.ANY)],
        out_specs=pl.BlockSpec((tm, d), lambda i, tab_: (i, 0)),
        scratch_shapes=[pltpu.VMEM((2, n_exp * COMBINE_SLOT, d), BF16),
                        pltpu.VMEM((tm, d), F32),
                        pltpu.SemaphoreType.DMA((2,))],
    )
    return pl.pallas_call(
        functools.partial(_combine_kernel, cap=cap, n_tile=n_tile),
        grid_spec=grid_spec,
        out_shape=jax.ShapeDtypeStruct((t, d), F32),
        compiler_params=_cparams(("arbitrary",)),
        name="ec_combine",
    )(tab, aff, tau, need, eqs, x1, mod3, y)


def _moe(h2p, aff, afft, x1, mod_g, w_gate, w_up, w_down, seq_len):
    t = aff.shape[0]
    n_exp = aff.shape[1]
    cap = CAPACITY_FACTOR * t // n_exp
    tm = COMBINE_TILE
    idx, tau, need, pref, eqpref = _route(afft.reshape(n_exp, t // LANES, LANES), cap)
    y = _expert_ffn(idx.reshape(-1), h2p, w_gate, w_up, w_down, cap, min(1024, cap), 256)
    rows_per_tile = tm // LANES
    tab = jnp.concatenate([pref[:, 0, ::rows_per_tile], jnp.full((n_exp, 1), cap, I32)], axis=1).reshape(-1)
    eqs = eqpref[:, 0, ::rows_per_tile].T.reshape(t // tm, 1, n_exp)
    return _combine(tab, aff, tau[:, 0, 0].reshape(1, n_exp), need[:, 0, 0].reshape(1, n_exp), eqs,
                    x1, mod_g, y, cap, seq_len, tm)


def _trunk(x, mod3, seq_base, prm):
    b, s, d = x.shape
    t = b * s
    x2 = x.reshape(t, d)
    mod_g = lax.slice_in_dim(mod3, seq_base, seq_base + b, axis=0)
    z = _in_projection(x2, mod_g, prm["norm1_g"], prm["w_in"], prm["q_g"], prm["k_g"], s, min(1024, s))
    attn_o = _attention(z, prm["sink"], s)
    x1, h2p, aff, afft = _post_mixer(attn_o, z, x2, mod_g, prm["norm2_g"], prm["ln_g"], prm["ln_b"], prm["ws"],
                                     prm["bs"], prm["w_oa"], prm["w_os"], prm["w_out"], prm["w_r2"], s, 256)
    out = _moe(h2p, aff, afft, x1, mod_g, prm["w_gate"], prm["w_up"], prm["w_down"], s)
    return out.reshape(b, s, d)


def kernel(x_prompt, x_sample, c_prompt, c_sample, w_ada, b_ada, norm1_g, norm2_g, w_in, q_norm_g, k_norm_g,
           attn_sink, sgu_ln_g, sgu_ln_b, sgu_w, sgu_b, w_o_attn, w_o_sgu, w_out, w_router, w_gate, w_up,
           w_down):
    assert w_ada.shape[0] == 1
    bp, bs_ = x_prompt.shape[0], x_sample.shape[0]
    assert bp + bs_ <= N_SEQ_PAD
    d = D_MODEL
    c_pad = jnp.zeros((N_SEQ_PAD, d), F32).at[:bp].set(c_prompt).at[bp:bp + bs_].set(c_sample)
    mod = _modulation(c_pad, w_ada[0], b_ada[0])
    mod3 = mod.reshape(N_SEQ_PAD, 1, 6 * d)

    w_r = w_router[0]
    w_r_hi = w_r.astype(BF16)
    w_r_lo = (w_r - w_r_hi.astype(F32)).astype(BF16)
    pad = ((0, 0), (0, LANES - N_EXPERTS))
    w_r2 = jnp.concatenate([jnp.pad(w_r_hi, pad), jnp.pad(w_r_lo, pad)], axis=1)
    prm = dict(
        norm1_g=norm1_g[0].reshape(1, d), norm2_g=norm2_g[0].reshape(1, d), w_in=w_in[0].astype(BF16),
        q_g=q_norm_g[0].reshape(1, HEAD_DIM), k_g=k_norm_g[0].reshape(1, HEAD_DIM), sink=attn_sink[0],
        ln_g=sgu_ln_g[0].reshape(1, d), ln_b=sgu_ln_b[0].reshape(1, d),
        ws=sgu_w[0].astype(BF16),
        bs=jnp.broadcast_to(sgu_b[0][:, :, None], (N_SGU_GROUPS, BLOCK, BLOCK)),
        w_oa=w_o_attn[0].astype(BF16), w_os=w_o_sgu[0].astype(BF16), w_out=w_out[0].astype(BF16),
        w_r2=w_r2, w_gate=w_gate[0], w_up=w_up[0], w_down=w_down[0])
    y_prompt = _trunk(x_prompt, mod3, 0, prm)
    y_sample = _trunk(x_sample, mod3, bp, prm)
    return (y_prompt, y_sample)
```

```python
import functools

import jax
import jax.numpy as jnp
from jax import lax
from jax.experimental import pallas as pl
from jax.experimental.pallas import tpu as pltpu

F32 = jnp.float32
BF16 = jnp.bfloat16
I32 = jnp.int32

D_MODEL = 2048
HEAD_DIM = 128
N_Q_HEADS = 16
N_KV_HEADS = 4
Q_PER_KV = N_Q_HEADS // N_KV_HEADS
KV_WIDTH = N_KV_HEADS * HEAD_DIM
BLOCK = 128
N_SGU_GROUPS = 16
N_EXPERTS = 16
CAPACITY_FACTOR = 2
D_FF = 2816
NORM_EPS = 1e-6
LOG2E = 1.4426950408889634
IN_WIDTH = 11264
N_SEQ_PAD = 8

LANES = 128
SUBLANES = 8
BF16_ROWS = 16
MXU_DIM = 256

COL_TILE = 1024
Q_TILES = (0, 2)
GELU_TILES = (2, 6)
SIG_TILES = (6, 10)
KV_TILE = 10
N_COL_TILES = IN_WIDTH // COL_TILE

INPROJ_ROW_CHUNK = 256
ATTN_Q_BLOCKS = 2
ROUTE_CHUNK_GROUP = 4
COMBINE_TILE = 256
COMBINE_SLOT = 64

VMEM_LIMIT = 56 * 1024 * 1024


def _cparams(sem):
    return pltpu.CompilerParams(dimension_semantics=sem, vmem_limit_bytes=VMEM_LIMIT)


def _gelu_tanh(x):
    c = 0.7978845608028654
    return 0.5 * x * (1.0 + jnp.tanh(c * (x + 0.044715 * (x * x * x))))


def _sigmoid(x):
    return 1.0 / (1.0 + jnp.exp(-x))


def _ones_where(cond, dtype=F32):
    return jnp.where(cond, 1.0, 0.0).astype(dtype)


def _mod_kernel(c_ref, w_ref, b_ref, o_ref):
    c = c_ref[...]
    s = c * _sigmoid(c)
    o_ref[...] = jnp.dot(s, w_ref[...], precision=lax.Precision.HIGHEST,
                         preferred_element_type=F32) + b_ref[...]


def _modulation(c_pad, w_ada, b_ada):
    n = w_ada.shape[1]
    tn = 1024
    return pl.pallas_call(
        _mod_kernel,
        grid=(n // tn,),
        in_specs=[pl.BlockSpec((N_SEQ_PAD, D_MODEL), lambda j: (0, 0)),
                  pl.BlockSpec((D_MODEL, tn), lambda j: (0, j)),
                  pl.BlockSpec((1, tn), lambda j: (0, j))],
        out_specs=pl.BlockSpec((N_SEQ_PAD, tn), lambda j: (0, j)),
        out_shape=jax.ShapeDtypeStruct((N_SEQ_PAD, n), F32),
        compiler_params=_cparams(("arbitrary",)),
        name="adaln_mod",
    )(c_pad, w_ada, b_ada.reshape(1, n))


def _inproj_kernel(x_ref, sh_ref, sc_ref, g_ref, w_ref, qg_ref, kg_ref, o_ref, h_ref):
    j = pl.program_id(1)

    @pl.when(j == 0)
    def _():
        x = x_ref[...]
        ms = jnp.mean(x * x, axis=-1, keepdims=True)
        y = x * lax.rsqrt(ms + NORM_EPS) * g_ref[...]
        h_ref[...] = (y * (1.0 + sc_ref[...]) + sh_ref[...]).astype(BF16)

    def head_norm(sub, g):
        ms = jnp.mean(sub * sub, axis=-1, keepdims=True)
        return sub * lax.rsqrt(ms + NORM_EPS) * g

    def chunked(epilogue):
        for r in range(x_ref.shape[0] // INPROJ_ROW_CHUNK):
            rows = pl.ds(r * INPROJ_ROW_CHUNK, INPROJ_ROW_CHUNK)
            epilogue(rows, jnp.dot(h_ref[rows, :], w_ref[...], preferred_element_type=F32))

    @pl.when(j < Q_TILES[1])
    def _():
        g = qg_ref[...] * (HEAD_DIM ** -0.5 * LOG2E)

        def epilogue(rows, acc):
            for hh in range(COL_TILE // HEAD_DIM):
                cols = slice(hh * HEAD_DIM, (hh + 1) * HEAD_DIM)
                o_ref[rows, cols] = head_norm(acc[:, cols], g).astype(BF16)
        chunked(epilogue)

    @pl.when((j >= GELU_TILES[0]) & (j < GELU_TILES[1]))
    def _():
        def epilogue(rows, acc):
            o_ref[rows, :] = _gelu_tanh(acc).astype(BF16)
        chunked(epilogue)

    @pl.when((j >= SIG_TILES[0]) & (j < SIG_TILES[1]))
    def _():
        def epilogue(rows, acc):
            o_ref[rows, :] = _sigmoid(acc).astype(BF16)
        chunked(epilogue)

    @pl.when(j == KV_TILE)
    def _():
        g = kg_ref[...]

        def epilogue(rows, acc):
            for hh in range(N_KV_HEADS):
                cols = slice(hh * HEAD_DIM, (hh + 1) * HEAD_DIM)
                o_ref[rows, cols] = head_norm(acc[:, cols], g).astype(BF16)
            o_ref[rows, KV_WIDTH:] = acc[:, KV_WIDTH:].astype(BF16)
        chunked(epilogue)


def _in_projection(x2, mod3, norm_g, w_in_b, q_g, k_g, seq_len, tm):
    t = x2.shape[0]
    seq = lambda i: (i * tm) // seq_len
    w_tile = lambda j: jnp.where(j < Q_TILES[1], j, jnp.where(j == KV_TILE, Q_TILES[1], j + 1))
    return pl.pallas_call(
        _inproj_kernel,
        grid=(t // tm, N_COL_TILES),
        in_specs=[pl.BlockSpec((tm, D_MODEL), lambda i, j: (i, 0)),
                  pl.BlockSpec((None, 1, D_MODEL), lambda i, j: (seq(i), 0, 0)),
                  pl.BlockSpec((None, 1, D_MODEL), lambda i, j: (seq(i), 0, 1)),
                  pl.BlockSpec((1, D_MODEL), lambda i, j: (0, 0)),
                  pl.BlockSpec((D_MODEL, COL_TILE), lambda i, j: (0, w_tile(j))),
                  pl.BlockSpec((1, HEAD_DIM), lambda i, j: (0, 0)),
                  pl.BlockSpec((1, HEAD_DIM), lambda i, j: (0, 0))],
        out_specs=pl.BlockSpec((tm, COL_TILE), lambda i, j: (i, j)),
        out_shape=jax.ShapeDtypeStruct((t, IN_WIDTH), BF16),
        scratch_shapes=[pltpu.VMEM((tm, D_MODEL), BF16)],
        compiler_params=_cparams(("arbitrary", "arbitrary")),
        name="norm_inproj",
    )(x2, mod3, mod3, norm_g, w_in_b, q_g, k_g)


def _alibi_slope(head):
    return 2.0 ** (-8.0 * (head + 1) / N_Q_HEADS)


def _attn_kernel(sink_ref, q_ref, kp_ref, kc_ref, kn_ref, vp_ref, vc_ref, vn_ref, o_ref, pen_ref, *,
                 steps_per_seq):
    n = pl.program_id(0)
    pos = n % steps_per_seq
    first = pos == 0
    last = pos == steps_per_seq - 1

    @pl.when(n == 0)
    def _():
        qi = lax.broadcasted_iota(I32, (BLOCK, 3 * BLOCK), 0)
        kj = lax.broadcasted_iota(I32, (BLOCK, 3 * BLOCK), 1)
        absrel = jnp.abs(kj - BLOCK - qi)
        in_window = absrel <= BLOCK
        absrel_f = absrel.astype(F32)
        masks = (in_window, in_window & (kj >= BLOCK), in_window & (kj < 2 * BLOCK))
        for v, valid in enumerate(masks):
            for hd in range(N_Q_HEADS):
                pen_ref[v, hd] = jnp.where(valid, (_alibi_slope(hd) * LOG2E) * absrel_f, jnp.inf)

    for b in range(ATTN_Q_BLOCKS):
        q_rows = slice(b * BLOCK, (b + 1) * BLOCK)
        variant = 0
        if b == 0:
            variant = jnp.where(first, 1, variant)
        if b == ATTN_Q_BLOCKS - 1:
            variant = jnp.where(last, 2, variant)
        for h in range(N_KV_HEADS):
            kv_cols = slice(h * HEAD_DIM, (h + 1) * HEAD_DIM)
            k_parts = [kp_ref[:, kv_cols]] + [kc_ref[c * BLOCK:(c + 1) * BLOCK, kv_cols]
                                              for c in range(ATTN_Q_BLOCKS)] + [kn_ref[:, kv_cols]]
            v_parts = [vp_ref[:, kv_cols]] + [vc_ref[c * BLOCK:(c + 1) * BLOCK, kv_cols]
                                              for c in range(ATTN_Q_BLOCKS)] + [vn_ref[:, kv_cols]]
            kcat = jnp.concatenate(k_parts[b:b + 3], axis=0)
            vcat = jnp.concatenate(v_parts[b:b + 3], axis=0)
            heads = [h * Q_PER_KV + g for g in range(Q_PER_KV)]
            cols = [slice(hd * HEAD_DIM, (hd + 1) * HEAD_DIM) for hd in heads]
            sinks = [sink_ref[hd] * LOG2E for hd in heads]
            s = [lax.dot_general(q_ref[q_rows, c], kcat, (((1,), (1,)), ((), ())), preferred_element_type=F32)
                 for c in cols]
            s = [sg - pen_ref[variant, hd] for sg, hd in zip(s, heads)]
            m = [jnp.maximum(jnp.max(sg, axis=-1, keepdims=True), sk) for sg, sk in zip(s, sinks)]
            p = [jnp.exp2(sg - mg) for sg, mg in zip(s, m)]
            denom = [jnp.sum(pg, axis=-1, keepdims=True) + jnp.exp2(sk - mg) for pg, sk, mg in zip(p, sinks, m)]
            o = [jnp.dot(pg.astype(BF16), vcat, preferred_element_type=F32) for pg in p]
            for c, og, dg in zip(cols, o, denom):
                o_ref[q_rows, c] = (og / dg).astype(BF16)


def _attention(z, sink, seq_len):
    t = z.shape[0]
    nb = t // BLOCK
    qb = ATTN_Q_BLOCKS
    k_col = (IN_WIDTH - 2 * KV_WIDTH) // KV_WIDTH
    v_col = k_col + 1
    halo = lambda col: [pl.BlockSpec((BLOCK, KV_WIDTH), lambda n, s: (jnp.maximum(qb * n - 1, 0), col)),
                        pl.BlockSpec((qb * BLOCK, KV_WIDTH), lambda n, s: (n, col)),
                        pl.BlockSpec((BLOCK, KV_WIDTH), lambda n, s: (jnp.minimum(qb * n + qb, nb - 1), col))]
    grid_spec = pltpu.PrefetchScalarGridSpec(
        num_scalar_prefetch=1,
        grid=(nb // qb,),
        in_specs=[pl.BlockSpec((qb * BLOCK, D_MODEL), lambda n, s: (n, 0))] + halo(k_col) + halo(v_col),
        out_specs=pl.BlockSpec((qb * BLOCK, D_MODEL), lambda n, s: (n, 0)),
        scratch_shapes=[pltpu.VMEM((3, N_Q_HEADS, BLOCK, 3 * BLOCK), F32)],
    )
    assert qb >= 2
    return pl.pallas_call(
        functools.partial(_attn_kernel, steps_per_seq=seq_len // (qb * BLOCK)),
        grid_spec=grid_spec,
        out_shape=jax.ShapeDtypeStruct((t, D_MODEL), BF16),
        compiler_params=_cparams(("arbitrary",)),
        name="window_attn",
    )(sink, z, z, z, z, z, z, z)


def _post_kernel(attn_ref, u_ref, vg_ref, ga_ref, gs_ref, x_ref, g1_ref, sh2_ref, sc2_ref, n2g_ref,
                 lng_ref, lnb_ref, ws_ref, bs_ref, woa_ref, wos_ref, wout_ref, wr_ref,
                 x1_ref, h2p_ref, aff_ref, afft_ref, sgu_ref):
    tm = x_ref.shape[0]
    vg = vg_ref[...].astype(F32)
    mu = jnp.mean(vg, axis=-1, keepdims=True)
    cen = vg - mu
    var = jnp.mean(cen * cen, axis=-1, keepdims=True)
    vn = (cen * lax.rsqrt(var + NORM_EPS) * lng_ref[...] + lnb_ref[...]).astype(BF16)
    for c in range(tm // BLOCK):
        rows = slice(c * BLOCK, (c + 1) * BLOCK)
        for g in range(N_SGU_GROUPS):
            cols = slice(g * BLOCK, (g + 1) * BLOCK)
            mixed = jnp.dot(ws_ref[g], vn[rows, cols], preferred_element_type=F32) + bs_ref[g]
            sgu_ref[rows, cols] = (u_ref[rows, cols].astype(F32) * mixed).astype(BF16)
    a = jnp.dot(attn_ref[...], woa_ref[...], preferred_element_type=F32)
    s = jnp.dot(sgu_ref[...], wos_ref[...], preferred_element_type=F32)
    merged = (ga_ref[...].astype(F32) * a + gs_ref[...].astype(F32) * s).astype(BF16)
    mix = jnp.dot(merged, wout_ref[...], preferred_element_type=F32)
    x1 = x_ref[...] + g1_ref[...] * mix
    x1_ref[...] = x1
    ms = jnp.mean(x1 * x1, axis=-1, keepdims=True)
    h2 = x1 * lax.rsqrt(ms + NORM_EPS) * n2g_ref[...]
    h2 = h2 * (1.0 + sc2_ref[...]) + sh2_ref[...]
    h2b = h2.astype(BF16)
    half = D_MODEL // 2
    n_slab = half // LANES
    lo = lax.shift_right_logical(lax.bitcast_convert_type(h2b[:, :half].astype(F32), I32), 16)
    hi = lax.bitcast_convert_type(h2b[:, half:].astype(F32), I32) & jnp.int32(-65536)
    word = hi | lo
    for j in range(n_slab):
        h2p_ref[pl.ds(j, tm, stride=n_slab), :] = word[:, j * LANES:(j + 1) * LANES]
    h_lo = (h2 - h2b.astype(F32)).astype(BF16)
    r1 = jnp.dot(h2b, wr_ref[...], preferred_element_type=F32)
    r2 = jnp.dot(h_lo, wr_ref[:, :LANES], preferred_element_type=F32)
    logits = r1[:, :LANES] + r1[:, LANES:] + r2
    lane = lax.broadcasted_iota(I32, logits.shape, 1)
    logits = jnp.where(lane < N_EXPERTS, logits, -jnp.inf)
    logits = logits - jnp.max(logits, axis=-1, keepdims=True)
    ex = jnp.exp(logits)
    aff = ex / jnp.sum(ex, axis=-1, keepdims=True)
    aff_ref[...] = aff[:, :N_EXPERTS]
    afft_ref[...] = aff.T[:N_EXPERTS, :]


def _post_mixer(attn_o, z, x2, mod3, norm2_g, ln_g, ln_b, ws, bs, w_oa, w_os, w_out, w_r2, seq_len, tm):
    t = x2.shape[0]
    n_slab = D_MODEL // 2 // LANES
    seq = lambda i: (i * tm) // seq_len
    const2 = lambda i: (0, 0)
    const3 = lambda i: (0, 0, 0)
    resident = lambda shape, imap: pl.BlockSpec(shape, imap, pipeline_mode=pl.Buffered(1))
    tok = lambda col: pl.BlockSpec((tm, D_MODEL), lambda i: (i, col))
    modv = lambda col: pl.BlockSpec((None, 1, D_MODEL), lambda i: (seq(i), 0, col))
    return pl.pallas_call(
        _post_kernel,
        grid=(t // tm,),
        in_specs=[tok(0),
                  tok(1), tok(2), tok(3), tok(4),
                  tok(0),
                  modv(2), modv(3), modv(4),
                  pl.BlockSpec((1, D_MODEL), const2),
                  pl.BlockSpec((1, D_MODEL), const2),
                  pl.BlockSpec((1, D_MODEL), const2),
                  resident((N_SGU_GROUPS, BLOCK, BLOCK), const3),
                  resident((N_SGU_GROUPS, BLOCK, BLOCK), const3),
                  resident((D_MODEL, D_MODEL), const2),
                  resident((D_MODEL, D_MODEL), const2),
                  resident((D_MODEL, D_MODEL), const2),
                  resident((D_MODEL, 2 * LANES), const2)],
        out_specs=[pl.BlockSpec((tm, D_MODEL), lambda i: (i, 0)),
                   pl.BlockSpec((tm * n_slab, LANES), lambda i: (i, 0)),
                   pl.BlockSpec((tm, N_EXPERTS), lambda i: (i, 0)),
                   pl.BlockSpec((N_EXPERTS, tm), lambda i: (0, i))],
        out_shape=[jax.ShapeDtypeStruct((t, D_MODEL), F32),
                   jax.ShapeDtypeStruct((t * n_slab, LANES), I32),
                   jax.ShapeDtypeStruct((t, N_EXPERTS), F32),
                   jax.ShapeDtypeStruct((N_EXPERTS, t), F32)],
        scratch_shapes=[pltpu.VMEM((tm, D_MODEL), BF16)],
        compiler_params=_cparams(("arbitrary",)),
        name="post_mixer",
    )(attn_o, z, z, z, z, x2, mod3, mod3, mod3, norm2_g, ln_g, ln_b, ws, bs, w_oa, w_os, w_out, w_r2)


def _route_kernel(aff_ref, idx_ref, tau_ref, need_ref, pref_ref, eqpref_ref, bits_ref, taus_ref, *, cap):
    n_exp, n_rows, _ = aff_ref.shape
    n_tok = n_rows * LANES
    bits_ref[...] = lax.bitcast_convert_type(aff_ref[...], I32)

    def bisect(i, v):
        cand = v | lax.shift_left(jnp.int32(1), 30 - i)
        ge = _ones_where(bits_ref[...] >= cand)
        cnt = jnp.sum(jnp.sum(ge, axis=1, keepdims=True), axis=2, keepdims=True)
        return jnp.where(cnt >= cap, cand, v)

    taus_ref[...] = lax.fori_loop(0, 31, bisect, jnp.zeros((n_exp, 1, LANES), I32))

    li = lax.broadcasted_iota(I32, (LANES, LANES), 0)
    lj = lax.broadcasted_iota(I32, (LANES, LANES), 1)
    upper_incl = _ones_where(li <= lj, BF16)
    ones_sq = jnp.ones((LANES, LANES), BF16)
    ri = lax.broadcasted_iota(I32, (n_rows, n_rows), 0)
    rj = lax.broadcasted_iota(I32, (n_rows, n_rows), 1)
    lower_strict = _ones_where(rj < ri, BF16)
    upper_strict = _ones_where(ri < rj, BF16)
    ones_rows = jnp.ones((SUBLANES, LANES), BF16)
    nt_dims = (((1,), (1,)), ((), ()))

    def incl_cumsum(mb):
        local = jnp.dot(mb, upper_incl, preferred_element_type=F32)
        totb = jnp.dot(mb, ones_sq, preferred_element_type=F32)
        prefc = jnp.dot(lower_strict, totb.astype(BF16), preferred_element_type=F32)
        return local + prefc

    def row_prefix(mb):
        tot_row = lax.dot_general(ones_rows, mb, nt_dims, preferred_element_type=F32)
        pref_row = jnp.dot(tot_row.astype(BF16), upper_strict, preferred_element_type=F32)
        return tot_row, pref_row

    def per_expert(e, carry):
        t = taus_ref[e]
        b = bits_ref[e]
        gt = b > t
        eq = b == t
        eqb = _ones_where(eq, BF16)
        need = cap - jnp.sum(_ones_where(gt))
        sel = gt | (eq & (incl_cumsum(eqb) <= need))
        m = _ones_where(sel, BF16)
        glob = incl_cumsum(m)
        tot_row, pref_row = row_prefix(m)
        incl_row = pref_row + tot_row
        _, eq_pref_row = row_prefix(eqb)
        pref_ref[e] = pref_row.astype(I32)
        eqpref_ref[e] = eq_pref_row
        tau_ref[e] = jnp.broadcast_to(lax.bitcast_convert_type(t, F32), (SUBLANES, LANES))
        need_ref[e] = jnp.full((SUBLANES, LANES), need, F32)
        ghi = jnp.floor(glob * (1.0 / MXU_DIM))
        glo = (glob - MXU_DIM * ghi).astype(BF16)
        ghi = ghi.astype(BF16)
        pr = pref_row[0:1, :]
        ir = incl_row[0:1, :]

        n_chunk = cap // LANES
        group = ROUTE_CHUNK_GROUP if n_chunk % ROUTE_CHUNK_GROUP == 0 else 1

        def chunks(cg, carry2):
            cs = [cg * group + k for k in range(group)]
            s_r = [(c * LANES + lax.broadcasted_iota(I32, (LANES, n_rows), 0)).astype(F32) for c in cs]
            onehot = [_ones_where((pr <= s) & (s < ir), BF16) for s in s_r]
            rowid = [jnp.sum(_ones_where(ir <= s), axis=-1, keepdims=True) for s in s_r]
            grow = [MXU_DIM * jnp.dot(oh, ghi, preferred_element_type=F32)
                    + jnp.dot(oh, glo, preferred_element_type=F32) for oh in onehot]
            s_l = [(c * LANES + lax.broadcasted_iota(I32, (LANES, LANES), 0)).astype(F32) for c in cs]
            inrow = [jnp.sum(_ones_where(gr <= s), axis=-1, keepdims=True) for gr, s in zip(grow, s_l)]
            tok = [jnp.minimum(r * LANES + q, n_tok - 1.0) for r, q in zip(rowid, inrow)]
            tok_t = [jnp.broadcast_to(t_, (LANES, LANES)).T for t_ in tok]
            for c, tt in zip(cs, tok_t):
                idx_ref[e, pl.ds(c, 1), :] = tt[0:1, :].astype(I32)
            return carry2

        lax.fori_loop(0, n_chunk // group, chunks, 0)
        return carry

    lax.fori_loop(0, n_exp, per_expert, 0)


def _route(aff3, cap):
    n_exp, n_rows, _ = aff3.shape
    rep = lambda dt, w: jax.ShapeDtypeStruct((n_exp, SUBLANES, w), dt)
    return pl.pallas_call(
        functools.partial(_route_kernel, cap=cap),
        out_shape=[jax.ShapeDtypeStruct((n_exp, cap // LANES, LANES), I32),
                   rep(F32, LANES), rep(F32, LANES), rep(I32, n_rows), rep(F32, n_rows)],
        scratch_shapes=[pltpu.VMEM((n_exp, n_rows, LANES), I32), pltpu.VMEM((n_exp, 1, LANES), I32)],
        compiler_params=pltpu.CompilerParams(vmem_limit_bytes=VMEM_LIMIT),
        name="ec_route",
    )(aff3)


def _ffn_kernel(idx_ref, h2p_hbm, wg_hbm, wu_hbm, wd_hbm, o_ref, xraw_ref, xb_ref, acc_ref, wgb_ref, wub_ref,
                wdb_ref, sem, wsem, *, tc, rows_per_step, slab, fc, nf):
    ct_n = pl.num_programs(1)
    expert = pl.program_id(0)
    tile = expert * ct_n + pl.program_id(1)
    n_tiles = pl.num_programs(0) * ct_n
    slot = tile % 2
    nxt = jnp.minimum(tile + 1, n_tiles - 1)
    w_parity = (tile * nf) % 2

    def row_copy(tile_id, s, dst_slot):
        tok = idx_ref[tile_id * tc + jnp.minimum(s, tc - 1)]
        return pltpu.make_async_copy(
            h2p_hbm.at[pl.ds(pl.multiple_of(tok * slab, slab), slab), :],
            xraw_ref.at[dst_slot, pl.ds(pl.multiple_of(s * slab, slab), slab), :], sem.at[dst_slot])

    def wait_slot(s_):
        pltpu.make_async_copy(xraw_ref.at[s_], xraw_ref.at[s_], sem.at[s_]).wait()

    def weight_copies(e, f, ws):
        cols = pl.ds(pl.multiple_of(f * fc, fc), fc)
        return (pltpu.make_async_copy(wg_hbm.at[e, :, cols], wgb_ref.at[ws], wsem.at[ws]),
                pltpu.make_async_copy(wu_hbm.at[e, :, cols], wub_ref.at[ws], wsem.at[ws]),
                pltpu.make_async_copy(wd_hbm.at[e, cols, :], wdb_ref.at[ws], wsem.at[ws]))

    @pl.when(tile == 0)
    def _():
        def body(s, carry):
            row_copy(tile, s, slot).start()
            return carry
        lax.fori_loop(0, rows_per_step * nf, body, 0)
        for c in weight_copies(expert, 0, w_parity):
            c.start()

    wait_slot(slot)
    half = slab * LANES
    for j in range(slab):
        w = xraw_ref[slot, pl.ds(j, tc, stride=slab), :]
        lo = lax.bitcast_convert_type(lax.shift_left(w, 16), F32)
        hi = lax.bitcast_convert_type(w & jnp.int32(-65536), F32)
        xb_ref[:, j * LANES:(j + 1) * LANES] = lo.astype(BF16)
        xb_ref[:, half + j * LANES:half + (j + 1) * LANES] = hi.astype(BF16)
    def chunk(f, is_first=False, is_last=False):
        ws = (w_parity + f) % 2
        for c in weight_copies(expert, f, ws):
            c.wait()
        if is_last:
            @pl.when(tile < n_tiles - 1)
            def _():
                for c in weight_copies(nxt // ct_n, 0, 1 - ws):
                    c.start()
        else:
            for c in weight_copies(expert, f + 1, 1 - ws):
                c.start()

        for u in range(rows_per_step):
            row_copy(nxt, f * rows_per_step + u, 1 - slot).start()

        x = xb_ref[...]
        g = jnp.dot(x, wgb_ref[ws].astype(BF16), preferred_element_type=F32)
        up = jnp.dot(x, wub_ref[ws].astype(BF16), preferred_element_type=F32)
        hmid = (g * _sigmoid(g) * up).astype(BF16)
        part = jnp.dot(hmid, wdb_ref[ws].astype(BF16), preferred_element_type=F32)
        if is_first:
            acc_ref[...] = part
        elif is_last:
            o_ref[...] = (acc_ref[...] + part).astype(BF16)
        else:
            acc_ref[...] += part

    def middle(f, carry):
        chunk(f)
        return carry

    chunk(0, is_first=True)
    lax.fori_loop(1, nf - 1, middle, 0)
    chunk(nf - 1, is_last=True)

    @pl.when(tile == n_tiles - 1)
    def _():
        wait_slot(1 - slot)


def _expert_ffn(idx_flat, h2p, w_gate, w_up, w_down, cap, tc, fc):
    n_exp, d, d_ff = w_gate.shape
    slab = d // 2 // LANES
    nct = cap // tc
    nf = d_ff // fc
    assert nf >= 2
    rows_per_step = -(-tc // nf)
    any_spec = pl.BlockSpec(memory_space=pl.ANY)
    grid_spec = pltpu.PrefetchScalarGridSpec(
        num_scalar_prefetch=1,
        grid=(n_exp, nct),
        in_specs=[any_spec, any_spec, any_spec, any_spec],
        out_specs=pl.BlockSpec((tc, d), lambda ei, ci, idx: (ei * nct + ci, 0)),
        scratch_shapes=[pltpu.VMEM((2, rows_per_step * nf * slab, LANES), I32),
                        pltpu.VMEM((tc, d), BF16),
                        pltpu.VMEM((tc, d), F32),
                        pltpu.VMEM((2, d, fc), w_gate.dtype),
                        pltpu.VMEM((2, d, fc), w_up.dtype),
                        pltpu.VMEM((2, fc, d), w_down.dtype),
                        pltpu.SemaphoreType.DMA((2,)),
                        pltpu.SemaphoreType.DMA((2,))],
    )
    return pl.pallas_call(
        functools.partial(_ffn_kernel, tc=tc, rows_per_step=rows_per_step, slab=slab, fc=fc, nf=nf),
        grid_spec=grid_spec,
        out_shape=jax.ShapeDtypeStruct((n_exp * cap, d), BF16),
        compiler_params=_cparams(("arbitrary", "arbitrary")),
        name="expert_ffn",
    )(idx_flat, h2p, w_gate, w_up, w_down)


def _combine_kernel(tab_ref, aff_ref, tau_ref, need_ref, eqs_ref, x1_ref, g2_ref, y_hbm, o_ref,
                    ybuf_ref, sem, *, cap, n_tile):
    i = pl.program_id(0)
    tm, n_exp = aff_ref.shape
    slot_rows = COMBINE_SLOT
    k_rows = n_exp * slot_rows
    total_rows = n_exp * cap
    lane_e = lax.broadcasted_iota(I32, (1, n_exp), 1)

    def geometry(tile, e):
        n0 = tab_ref[e * (n_tile + 1) + tile]
        n1 = tab_ref[e * (n_tile + 1) + tile + 1]
        first = e * cap + n0
        aligned = (first // BF16_ROWS) * BF16_ROWS
        return aligned, first - aligned, n1 - n0

    def round_src(aligned, q):
        src = aligned + q * slot_rows
        clamped = jnp.minimum(src, total_rows - slot_rows)
        return clamped, src - clamped

    def issue_round(tile, q, buf):
        for e in range(n_exp):
            aligned, _, _ = geometry(tile, e)
            src, _ = round_src(aligned, q)
            pltpu.make_async_copy(y_hbm.at[pl.ds(pl.multiple_of(src, BF16_ROWS), slot_rows), :],
                                  ybuf_ref.at[buf, pl.ds(e * slot_rows, slot_rows), :], sem.at[buf]).start()

    buf = i % 2

    @pl.when(i == 0)
    def _():
        issue_round(i, 0, buf)

    @pl.when(i + 1 < n_tile)
    def _():
        issue_round(i + 1, 0, 1 - buf)

    a = aff_ref[...]
    tau = tau_ref[...]
    eq = a == tau
    ti = lax.broadcasted_iota(I32, (tm, tm), 0)
    tj = lax.broadcasted_iota(I32, (tm, tm), 1)
    eq_rank = eqs_ref[...] + jnp.dot(_ones_where(tj <= ti, BF16), _ones_where(eq, BF16),
                                     preferred_element_type=F32)
    sel = (a > tau) | (eq & (eq_rank <= need_ref[...]))
    wm = jnp.where(sel, a, 0.0).astype(BF16)
    rank = jnp.dot(_ones_where(tj < ti, BF16), _ones_where(sel, BF16), preferred_element_type=F32)

    n_round = jnp.int32(1)
    for e in range(n_exp):
        _, delta, n_sel = geometry(i, e)
        n_round = jnp.maximum(n_round, (delta + n_sel + slot_rows - 1) // slot_rows)

    spread = _ones_where(lax.broadcasted_iota(I32, (n_exp, k_rows), 1) // slot_rows
                         == lax.broadcasted_iota(I32, (n_exp, k_rows), 0), BF16)
    col_in_slot = (lax.broadcasted_iota(I32, (tm, k_rows), 1) % slot_rows).astype(F32)
    wm_cols = jnp.dot(wm, spread, preferred_element_type=F32)

    def round_sum(q):
        pltpu.make_async_copy(ybuf_ref.at[buf], ybuf_ref.at[buf], sem.at[buf]).wait()
        offset = jnp.zeros((1, n_exp), F32)
        shift = jnp.zeros((1, n_exp), F32)
        for e in range(n_exp):
            aligned, delta, _ = geometry(i, e)
            _, sh = round_src(aligned, q)
            offset = jnp.where(lane_e == e, (delta - q * slot_rows).astype(F32), offset)
            shift = jnp.where(lane_e == e, sh.astype(F32), shift)
        u = rank + offset
        pos = jnp.where(u >= 0, u + shift, -1.0).astype(BF16)
        pos_cols = jnp.dot(pos, spread, preferred_element_type=F32)
        place = jnp.where(pos_cols == col_in_slot, wm_cols, 0.0).astype(BF16)
        return jnp.dot(place, ybuf_ref[buf], preferred_element_type=F32)

    o_ref[...] = x1_ref[...] + g2_ref[...] * round_sum(0)

    def extra_round(q, carry):
        issue_round(i, q, buf)
        o_ref[...] += g2_ref[...] * round_sum(q)
        return carry

    lax.fori_loop(1, n_round, extra_round, 0)


def _combine(tab, aff, tau, need, eqs, x1, mod3, y, cap, seq_len, tm):
    t, d = x1.shape
    n_exp = aff.shape[1]
    n_tile = t // tm
    seq = lambda i, tab_: ((i * tm) // seq_len, 0, 5)
    grid_spec = pltpu.PrefetchScalarGridSpec(
        num_scalar_prefetch=1,
        grid=(n_tile,),
        in_specs=[pl.BlockSpec((tm, n_exp), lambda i, tab_: (i, 0)),
                  pl.BlockSpec((1, n_exp), lambda i, tab_: (0, 0)),
                  pl.BlockSpec((1, n_exp), lambda i, tab_: (0, 0)),
                  pl.BlockSpec((None, 1, n_exp), lambda i, tab_: (i, 0, 0)),
                  pl.BlockSpec((tm, d), lambda i, tab_: (i, 0)),
                  pl.BlockSpec((None, 1, d), seq),
                  pl.BlockSpec(memory_space=pl.ANY)],
        out_specs=pl.BlockSpec((tm, d), lambda i, tab_: (i, 0)),
        scratch_shapes=[pltpu.VMEM((2, n_exp * COMBINE_SLOT, d), BF16),
                        pltpu.SemaphoreType.DMA((2,))],
    )
    return pl.pallas_call(
        functools.partial(_combine_kernel, cap=cap, n_tile=n_tile),
        grid_spec=grid_spec,
        out_shape=jax.ShapeDtypeStruct((t, d), F32),
        compiler_params=_cparams(("arbitrary",)),
        name="ec_combine",
    )(tab, aff, tau, need, eqs, x1, mod3, y)


def _moe(h2p, aff, afft, x1, mod_g, w_gate, w_up, w_down, seq_len):
    t = aff.shape[0]
    n_exp = aff.shape[1]
    cap = CAPACITY_FACTOR * t // n_exp
    tm = COMBINE_TILE
    idx, tau, need, pref, eqpref = _route(afft.reshape(n_exp, t // LANES, LANES), cap)
    y = _expert_ffn(idx.reshape(-1), h2p, w_gate, w_up, w_down, cap, min(1024, cap), 256)
    rows_per_tile = tm // LANES
    tab = jnp.concatenate([pref[:, 0, ::rows_per_tile], jnp.full((n_exp, 1), cap, I32)], axis=1).reshape(-1)
    eqs = eqpref[:, 0, ::rows_per_tile].T.reshape(t // tm, 1, n_exp)
    return _combine(tab, aff, tau[:, 0, 0].reshape(1, n_exp), need[:, 0, 0].reshape(1, n_exp), eqs,
                    x1, mod_g, y, cap, seq_len, tm)


def _trunk(x, mod3, seq_base, prm):
    b, s, d = x.shape
    t = b * s
    x2 = x.reshape(t, d)
    mod_g = lax.slice_in_dim(mod3, seq_base, seq_base + b, axis=0)
    z = _in_projection(x2, mod_g, prm["norm1_g"], prm["w_in"], prm["q_g"], prm["k_g"], s, min(1024, s))
    attn_o = _attention(z, prm["sink"], s)
    x1, h2p, aff, afft = _post_mixer(attn_o, z, x2, mod_g, prm["norm2_g"], prm["ln_g"], prm["ln_b"], prm["ws"],
                                     prm["bs"], prm["w_oa"], prm["w_os"], prm["w_out"], prm["w_r2"], s, 256)
    out = _moe(h2p, aff, afft, x1, mod_g, prm["w_gate"], prm["w_up"], prm["w_down"], s)
    return out.reshape(b, s, d)


def kernel(x_prompt, x_sample, c_prompt, c_sample, w_ada, b_ada, norm1_g, norm2_g, w_in, q_norm_g, k_norm_g,
           attn_sink, sgu_ln_g, sgu_ln_b, sgu_w, sgu_b, w_o_attn, w_o_sgu, w_out, w_router, w_gate, w_up,
           w_down):
    assert w_ada.shape[0] == 1
    bp, bs_ = x_prompt.shape[0], x_sample.shape[0]
    assert bp + bs_ <= N_SEQ_PAD
    d = D_MODEL
    c_pad = jnp.zeros((N_SEQ_PAD, d), F32).at[:bp].set(c_prompt).at[bp:bp + bs_].set(c_sample)
    mod = _modulation(c_pad, w_ada[0], b_ada[0])
    mod3 = mod.reshape(N_SEQ_PAD, 1, 6 * d)

    w_r = w_router[0]
    w_r_hi = w_r.astype(BF16)
    w_r_lo = (w_r - w_r_hi.astype(F32)).astype(BF16)
    pad = ((0, 0), (0, LANES - N_EXPERTS))
    w_r2 = jnp.concatenate([jnp.pad(w_r_hi, pad), jnp.pad(w_r_lo, pad)], axis=1)
    prm = dict(
        norm1_g=norm1_g[0].reshape(1, d), norm2_g=norm2_g[0].reshape(1, d), w_in=w_in[0].astype(BF16),
        q_g=q_norm_g[0].reshape(1, HEAD_DIM), k_g=k_norm_g[0].reshape(1, HEAD_DIM), sink=attn_sink[0],
        ln_g=sgu_ln_g[0].reshape(1, d), ln_b=sgu_ln_b[0].reshape(1, d),
        ws=sgu_w[0].astype(BF16),
        bs=jnp.broadcast_to(sgu_b[0][:, :, None], (N_SGU_GROUPS, BLOCK, BLOCK)),
        w_oa=w_o_attn[0].astype(BF16), w_os=w_o_sgu[0].astype(BF16), w_out=w_out[0].astype(BF16),
        w_r2=w_r2, w_gate=w_gate[0], w_up=w_up[0], w_down=w_down[0])
    y_prompt = _trunk(x_prompt, mod3, 0, prm)
    y_sample = _trunk(x_sample, mod3, bp, prm)
    return (y_prompt, y_sample)
```

```python
import functools

import jax
import jax.numpy as jnp
from jax import lax
from jax.experimental import pallas as pl
from jax.experimental.pallas import tpu as pltpu

F32 = jnp.float32
BF16 = jnp.bfloat16
I32 = jnp.int32

D_MODEL = 2048
HEAD_DIM = 128
N_Q_HEADS = 16
N_KV_HEADS = 4
Q_PER_KV = N_Q_HEADS // N_KV_HEADS
KV_WIDTH = N_KV_HEADS * HEAD_DIM
BLOCK = 128
N_SGU_GROUPS = 16
N_EXPERTS = 16
CAPACITY_FACTOR = 2
D_FF = 2816
NORM_EPS = 1e-6
LOG2E = 1.4426950408889634
IN_WIDTH = 11264
N_SEQ_PAD = 8

LANES = 128
SUBLANES = 8
BF16_ROWS = 16
MXU_DIM = 256

COL_TILE = 1024
Q_TILES = (0, 2)
GELU_TILES = (2, 6)
SIG_TILES = (6, 10)
KV_TILE = 10
N_COL_TILES = IN_WIDTH // COL_TILE

INPROJ_ROW_CHUNK = 512
ATTN_Q_BLOCKS = 2
ROUTE_CHUNK_GROUP = 4
COMBINE_TILE = 256
COMBINE_SLOT = 64

VMEM_LIMIT = 56 * 1024 * 1024


def _cparams(sem):
    return pltpu.CompilerParams(dimension_semantics=sem, vmem_limit_bytes=VMEM_LIMIT)


def _gelu_tanh(x):
    c = 0.7978845608028654
    return 0.5 * x * (1.0 + jnp.tanh(c * (x + 0.044715 * (x * x * x))))


def _sigmoid(x):
    return 1.0 / (1.0 + jnp.exp(-x))


def _ones_where(cond, dtype=F32):
    return jnp.where(cond, 1.0, 0.0).astype(dtype)


def _mod_kernel(c_ref, w_ref, b_ref, o_ref):
    c = c_ref[...]
    s = c * _sigmoid(c)
    o_ref[...] = jnp.dot(s, w_ref[...], precision=lax.Precision.HIGHEST,
                         preferred_element_type=F32) + b_ref[...]


def _modulation(c_pad, w_ada, b_ada):
    n = w_ada.shape[1]
    tn = 1024
    return pl.pallas_call(
        _mod_kernel,
        grid=(n // tn,),
        in_specs=[pl.BlockSpec((N_SEQ_PAD, D_MODEL), lambda j: (0, 0)),
                  pl.BlockSpec((D_MODEL, tn), lambda j: (0, j)),
                  pl.BlockSpec((1, tn), lambda j: (0, j))],
        out_specs=pl.BlockSpec((N_SEQ_PAD, tn), lambda j: (0, j)),
        out_shape=jax.ShapeDtypeStruct((N_SEQ_PAD, n), F32),
        compiler_params=_cparams(("arbitrary",)),
        name="adaln_mod",
    )(c_pad, w_ada, b_ada.reshape(1, n))


def _inproj_kernel(x_ref, sh_ref, sc_ref, g_ref, w_ref, qg_ref, kg_ref, o_ref, h_ref):
    i = pl.program_id(0)
    j = pl.program_id(1)
    slot = i % 2
    n_chunk = x_ref.shape[0] // INPROJ_ROW_CHUNK

    def norm_rows(rows, dst_slot):
        x = x_ref[rows, :]
        ms = jnp.mean(x * x, axis=-1, keepdims=True)
        y = x * lax.rsqrt(ms + NORM_EPS) * g_ref[...]
        h_ref[dst_slot, rows, :] = (y * (1.0 + sc_ref[...]) + sh_ref[...]).astype(BF16)

    @pl.when((i == 0) & (j == 0))
    def _():
        for r in range(n_chunk):
            norm_rows(pl.ds(r * INPROJ_ROW_CHUNK, INPROJ_ROW_CHUNK), slot)

    def head_norm(sub, g):
        ms = jnp.mean(sub * sub, axis=-1, keepdims=True)
        return sub * lax.rsqrt(ms + NORM_EPS) * g

    def chunked(epilogue, side=None):
        for r in range(n_chunk):
            rows = pl.ds(r * INPROJ_ROW_CHUNK, INPROJ_ROW_CHUNK)
            epilogue(rows, jnp.dot(h_ref[slot, rows, :], w_ref[...], preferred_element_type=F32))
            if side is not None:
                side(rows)

    @pl.when(j < Q_TILES[1])
    def _():
        g = qg_ref[...] * (HEAD_DIM ** -0.5 * LOG2E)

        def epilogue(rows, acc):
            for hh in range(COL_TILE // HEAD_DIM):
                cols = slice(hh * HEAD_DIM, (hh + 1) * HEAD_DIM)
                o_ref[rows, cols] = head_norm(acc[:, cols], g).astype(BF16)
        chunked(epilogue)

    @pl.when((j >= GELU_TILES[0]) & (j < GELU_TILES[1]))
    def _():
        def epilogue(rows, acc):
            o_ref[rows, :] = _gelu_tanh(acc).astype(BF16)
        chunked(epilogue)

    @pl.when((j >= SIG_TILES[0]) & (j < SIG_TILES[1]))
    def _():
        def epilogue(rows, acc):
            o_ref[rows, :] = _sigmoid(acc).astype(BF16)
        chunked(epilogue)

    @pl.when(j == KV_TILE)
    def _():
        g = kg_ref[...]

        def epilogue(rows, acc):
            for hh in range(N_KV_HEADS):
                cols = slice(hh * HEAD_DIM, (hh + 1) * HEAD_DIM)
                o_ref[rows, cols] = head_norm(acc[:, cols], g).astype(BF16)
            o_ref[rows, KV_WIDTH:] = acc[:, KV_WIDTH:].astype(BF16)
        chunked(epilogue, side=lambda rows: norm_rows(rows, 1 - slot))


def _in_projection(x2, mod3, norm_g, w_in_b, q_g, k_g, seq_len, tm):
    t = x2.shape[0]
    n_row_tiles = t // tm
    assert KV_TILE == N_COL_TILES - 1
    x_tile = lambda i, j: jnp.minimum(i + jnp.where(j == N_COL_TILES - 1, 1, 0), n_row_tiles - 1)
    seq = lambda i, j: (x_tile(i, j) * tm) // seq_len
    w_tile = lambda j: jnp.where(j < Q_TILES[1], j, jnp.where(j == KV_TILE, Q_TILES[1], j + 1))
    return pl.pallas_call(
        _inproj_kernel,
        grid=(n_row_tiles, N_COL_TILES),
        in_specs=[pl.BlockSpec((tm, D_MODEL), lambda i, j: (x_tile(i, j), 0)),
                  pl.BlockSpec((None, 1, D_MODEL), lambda i, j: (seq(i, j), 0, 0)),
                  pl.BlockSpec((None, 1, D_MODEL), lambda i, j: (seq(i, j), 0, 1)),
                  pl.BlockSpec((1, D_MODEL), lambda i, j: (0, 0)),
                  pl.BlockSpec((D_MODEL, COL_TILE), lambda i, j: (0, w_tile(j))),
                  pl.BlockSpec((1, HEAD_DIM), lambda i, j: (0, 0)),
                  pl.BlockSpec((1, HEAD_DIM), lambda i, j: (0, 0))],
        out_specs=pl.BlockSpec((tm, COL_TILE), lambda i, j: (i, j)),
        out_shape=jax.ShapeDtypeStruct((t, IN_WIDTH), BF16),
        scratch_shapes=[pltpu.VMEM((2, tm, D_MODEL), BF16)],
        compiler_params=_cparams(("arbitrary", "arbitrary")),
        name="norm_inproj",
    )(x2, mod3, mod3, norm_g, w_in_b, q_g, k_g)


def _alibi_slope(head):
    return 2.0 ** (-8.0 * (head + 1) / N_Q_HEADS)


def _attn_kernel(sink_ref, q_ref, kp_ref, kc_ref, kn_ref, vp_ref, vc_ref, vn_ref, o_ref, pen_ref, *,
                 steps_per_seq):
    n = pl.program_id(0)
    pos = n % steps_per_seq
    first = pos == 0
    last = pos == steps_per_seq - 1

    @pl.when(n == 0)
    def _():
        qi = lax.broadcasted_iota(I32, (BLOCK, 3 * BLOCK), 0)
        kj = lax.broadcasted_iota(I32, (BLOCK, 3 * BLOCK), 1)
        absrel = jnp.abs(kj - BLOCK - qi)
        in_window = absrel <= BLOCK
        absrel_f = absrel.astype(F32)
        masks = (in_window, in_window & (kj >= BLOCK), in_window & (kj < 2 * BLOCK))
        for v, valid in enumerate(masks):
            for hd in range(N_Q_HEADS):
                pen_ref[v, hd] = jnp.where(valid, (_alibi_slope(hd) * LOG2E) * absrel_f, jnp.inf)

    for b in range(ATTN_Q_BLOCKS):
        q_rows = slice(b * BLOCK, (b + 1) * BLOCK)
        variant = 0
        if b == 0:
            variant = jnp.where(first, 1, variant)
        if b == ATTN_Q_BLOCKS - 1:
            variant = jnp.where(last, 2, variant)
        for h in range(N_KV_HEADS):
            kv_cols = slice(h * HEAD_DIM, (h + 1) * HEAD_DIM)
            k_parts = [kp_ref[:, kv_cols]] + [kc_ref[c * BLOCK:(c + 1) * BLOCK, kv_cols]
                                              for c in range(ATTN_Q_BLOCKS)] + [kn_ref[:, kv_cols]]
            v_parts = [vp_ref[:, kv_cols]] + [vc_ref[c * BLOCK:(c + 1) * BLOCK, kv_cols]
                                              for c in range(ATTN_Q_BLOCKS)] + [vn_ref[:, kv_cols]]
            kcat = jnp.concatenate(k_parts[b:b + 3], axis=0)
            vcat = jnp.concatenate(v_parts[b:b + 3], axis=0)
            heads = [h * Q_PER_KV + g for g in range(Q_PER_KV)]
            cols = [slice(hd * HEAD_DIM, (hd + 1) * HEAD_DIM) for hd in heads]
            sinks = [sink_ref[hd] * LOG2E for hd in heads]
            s = [lax.dot_general(q_ref[q_rows, c], kcat, (((1,), (1,)), ((), ())), preferred_element_type=F32)
                 for c in cols]
            s = [sg - pen_ref[variant, hd] for sg, hd in zip(s, heads)]
            m = [jnp.maximum(jnp.max(sg, axis=-1, keepdims=True), sk) for sg, sk in zip(s, sinks)]
            p = [jnp.exp2(sg - mg) for sg, mg in zip(s, m)]
            denom = [jnp.sum(pg, axis=-1, keepdims=True) + jnp.exp2(sk - mg) for pg, sk, mg in zip(p, sinks, m)]
            o = [jnp.dot(pg.astype(BF16), vcat, preferred_element_type=F32) for pg in p]
            for c, og, dg in zip(cols, o, denom):
                o_ref[q_rows, c] = (og / dg).astype(BF16)


def _attention(z, sink, seq_len):
    t = z.shape[0]
    nb = t // BLOCK
    qb = ATTN_Q_BLOCKS
    k_col = (IN_WIDTH - 2 * KV_WIDTH) // KV_WIDTH
    v_col = k_col + 1
    halo = lambda col: [pl.BlockSpec((BLOCK, KV_WIDTH), lambda n, s: (jnp.maximum(qb * n - 1, 0), col)),
                        pl.BlockSpec((qb * BLOCK, KV_WIDTH), lambda n, s: (n, col)),
                        pl.BlockSpec((BLOCK, KV_WIDTH), lambda n, s: (jnp.minimum(qb * n + qb, nb - 1), col))]
    grid_spec = pltpu.PrefetchScalarGridSpec(
        num_scalar_prefetch=1,
        grid=(nb // qb,),
        in_specs=[pl.BlockSpec((qb * BLOCK, D_MODEL), lambda n, s: (n, 0))] + halo(k_col) + halo(v_col),
        out_specs=pl.BlockSpec((qb * BLOCK, D_MODEL), lambda n, s: (n, 0)),
        scratch_shapes=[pltpu.VMEM((3, N_Q_HEADS, BLOCK, 3 * BLOCK), F32)],
    )
    assert qb >= 2
    return pl.pallas_call(
        functools.partial(_attn_kernel, steps_per_seq=seq_len // (qb * BLOCK)),
        grid_spec=grid_spec,
        out_shape=jax.ShapeDtypeStruct((t, D_MODEL), BF16),
        compiler_params=_cparams(("arbitrary",)),
        name="window_attn",
    )(sink, z, z, z, z, z, z, z)


def _post_kernel(attn_ref, u_ref, vg_ref, ga_ref, gs_ref, x_ref, g1_ref, sh2_ref, sc2_ref, n2g_ref,
                 lng_ref, lnb_ref, ws_ref, bs_ref, woa_ref, wos_ref, wout_ref, wr_ref,
                 x1_ref, h2p_ref, aff_ref, afft_ref, sgu_ref):
    step = pl.program_id(0)
    tm = x_ref.shape[0]

    def prepare(dst):
        vg = vg_ref[...].astype(F32)
        mu = jnp.mean(vg, axis=-1, keepdims=True)
        cen = vg - mu
        var = jnp.mean(cen * cen, axis=-1, keepdims=True)
        vn = (cen * lax.rsqrt(var + NORM_EPS) * lng_ref[...] + lnb_ref[...]).astype(BF16)
        for c in range(tm // BLOCK):
            rows = slice(c * BLOCK, (c + 1) * BLOCK)
            for g in range(N_SGU_GROUPS):
                cols = slice(g * BLOCK, (g + 1) * BLOCK)
                mixed = jnp.dot(ws_ref[g], vn[rows, cols], preferred_element_type=F32) + bs_ref[g]
                sgu_ref[dst, rows, cols] = (u_ref[rows, cols].astype(F32) * mixed).astype(BF16)

    @pl.when(step == 0)
    def _():
        prepare(0)

    @pl.when(step > 0)
    def _():
        prepare(step % 2)
        _post_finish(attn_ref, ga_ref, gs_ref, x_ref, g1_ref, sh2_ref, sc2_ref, n2g_ref, woa_ref, wos_ref,
                     wout_ref, wr_ref, x1_ref, h2p_ref, aff_ref, afft_ref, sgu_ref.at[(step + 1) % 2])


def _post_finish(attn_ref, ga_ref, gs_ref, x_ref, g1_ref, sh2_ref, sc2_ref, n2g_ref, woa_ref, wos_ref,
                 wout_ref, wr_ref, x1_ref, h2p_ref, aff_ref, afft_ref, sgu_ref):
    tm = x_ref.shape[0]
    a = jnp.dot(attn_ref[...], woa_ref[...], preferred_element_type=F32)
    s = jnp.dot(sgu_ref[...], wos_ref[...], preferred_element_type=F32)
    merged = (ga_ref[...].astype(F32) * a + gs_ref[...].astype(F32) * s).astype(BF16)
    mix = jnp.dot(merged, wout_ref[...], preferred_element_type=F32)
    x1 = x_ref[...] + g1_ref[...] * mix
    x1_ref[...] = x1
    ms = jnp.mean(x1 * x1, axis=-1, keepdims=True)
    h2 = x1 * lax.rsqrt(ms + NORM_EPS) * n2g_ref[...]
    h2 = h2 * (1.0 + sc2_ref[...]) + sh2_ref[...]
    h2b = h2.astype(BF16)
    half = D_MODEL // 2
    n_slab = half // LANES
    lo = lax.shift_right_logical(lax.bitcast_convert_type(h2b[:, :half].astype(F32), I32), 16)
    hi = lax.bitcast_convert_type(h2b[:, half:].astype(F32), I32) & jnp.int32(-65536)
    word = hi | lo
    for j in range(n_slab):
        h2p_ref[pl.ds(j, tm, stride=n_slab), :] = word[:, j * LANES:(j + 1) * LANES]
    h_lo = (h2 - h2b.astype(F32)).astype(BF16)
    r1 = jnp.dot(h2b, wr_ref[...], preferred_element_type=F32)
    r2 = jnp.dot(h_lo, wr_ref[:, :LANES], preferred_element_type=F32)
    logits = r1[:, :LANES] + r1[:, LANES:] + r2
    lane = lax.broadcasted_iota(I32, logits.shape, 1)
    logits = jnp.where(lane < N_EXPERTS, logits, -jnp.inf)
    logits = logits - jnp.max(logits, axis=-1, keepdims=True)
    ex = jnp.exp(logits)
    aff = ex / jnp.sum(ex, axis=-1, keepdims=True)
    aff_ref[...] = aff[:, :N_EXPERTS]
    afft_ref[...] = aff.T[:N_EXPERTS, :]


def _post_mixer(attn_o, z, x2, mod3, norm2_g, ln_g, ln_b, ws, bs, w_oa, w_os, w_out, w_r2, seq_len, tm):
    t = x2.shape[0]
    n_slab = D_MODEL // 2 // LANES
    n_tiles = t // tm
    ahead = lambda s: jnp.minimum(s, n_tiles - 1)
    done = lambda s: jnp.maximum(s - 1, 0)
    seq = lambda s: (done(s) * tm) // seq_len
    const2 = lambda s: (0, 0)
    const3 = lambda s: (0, 0, 0)
    resident = lambda shape, imap: pl.BlockSpec(shape, imap, pipeline_mode=pl.Buffered(1))
    tok = lambda col: pl.BlockSpec((tm, D_MODEL), lambda s: (done(s), col))
    tok_ahead = lambda col: pl.BlockSpec((tm, D_MODEL), lambda s: (ahead(s), col))
    modv = lambda col: pl.BlockSpec((None, 1, D_MODEL), lambda s: (seq(s), 0, col))
    return pl.pallas_call(
        _post_kernel,
        grid=(n_tiles + 1,),
        in_specs=[tok(0),
                  tok_ahead(1), tok_ahead(2),
                  tok(3), tok(4),
                  tok(0),
                  modv(2), modv(3), modv(4),
                  pl.BlockSpec((1, D_MODEL), const2),
                  pl.BlockSpec((1, D_MODEL), const2),
                  pl.BlockSpec((1, D_MODEL), const2),
                  resident((N_SGU_GROUPS, BLOCK, BLOCK), const3),
                  resident((N_SGU_GROUPS, BLOCK, BLOCK), const3),
                  resident((D_MODEL, D_MODEL), const2),
                  resident((D_MODEL, D_MODEL), const2),
                  resident((D_MODEL, D_MODEL), const2),
                  resident((D_MODEL, 2 * LANES), const2)],
        out_specs=[pl.BlockSpec((tm, D_MODEL), lambda s: (done(s), 0)),
                   pl.BlockSpec((tm * n_slab, LANES), lambda s: (done(s), 0)),
                   pl.BlockSpec((tm, N_EXPERTS), lambda s: (done(s), 0)),
                   pl.BlockSpec((N_EXPERTS, tm), lambda s: (0, done(s)))],
        out_shape=[jax.ShapeDtypeStruct((t, D_MODEL), F32),
                   jax.ShapeDtypeStruct((t * n_slab, LANES), I32),
                   jax.ShapeDtypeStruct((t, N_EXPERTS), F32),
                   jax.ShapeDtypeStruct((N_EXPERTS, t), F32)],
        scratch_shapes=[pltpu.VMEM((2, tm, D_MODEL), BF16)],
        compiler_params=_cparams(("arbitrary",)),
        name="post_mixer",
    )(attn_o, z, z, z, z, x2, mod3, mod3, mod3, norm2_g, ln_g, ln_b, ws, bs, w_oa, w_os, w_out, w_r2)


def _route_kernel(aff_ref, idx_ref, tau_ref, need_ref, pref_ref, eqpref_ref, bits_ref, taus_ref, *, cap):
    n_exp, n_rows, _ = aff_ref.shape
    n_tok = n_rows * LANES
    bits_ref[...] = lax.bitcast_convert_type(aff_ref[...], I32)

    def bisect(i, v):
        cand = v | lax.shift_left(jnp.int32(1), 30 - i)
        ge = _ones_where(bits_ref[...] >= cand)
        cnt = jnp.sum(jnp.sum(ge, axis=1, keepdims=True), axis=2, keepdims=True)
        return jnp.where(cnt >= cap, cand, v)

    taus_ref[...] = lax.fori_loop(0, 31, bisect, jnp.zeros((n_exp, 1, LANES), I32))

    li = lax.broadcasted_iota(I32, (LANES, LANES), 0)
    lj = lax.broadcasted_iota(I32, (LANES, LANES), 1)
    upper_incl = _ones_where(li <= lj, BF16)
    ones_sq = jnp.ones((LANES, LANES), BF16)
    ri = lax.broadcasted_iota(I32, (n_rows, n_rows), 0)
    rj = lax.broadcasted_iota(I32, (n_rows, n_rows), 1)
    lower_strict = _ones_where(rj < ri, BF16)
    upper_strict = _ones_where(ri < rj, BF16)
    ones_rows = jnp.ones((SUBLANES, LANES), BF16)
    nt_dims = (((1,), (1,)), ((), ()))

    def incl_cumsum(mb):
        local = jnp.dot(mb, upper_incl, preferred_element_type=F32)
        totb = jnp.dot(mb, ones_sq, preferred_element_type=F32)
        prefc = jnp.dot(lower_strict, totb.astype(BF16), preferred_element_type=F32)
        return local + prefc

    def row_prefix(mb):
        tot_row = lax.dot_general(ones_rows, mb, nt_dims, preferred_element_type=F32)
        pref_row = jnp.dot(tot_row.astype(BF16), upper_strict, preferred_element_type=F32)
        return tot_row, pref_row

    def per_expert(e, carry):
        t = taus_ref[e]
        b = bits_ref[e]
        gt = b > t
        eq = b == t
        eqb = _ones_where(eq, BF16)
        need = cap - jnp.sum(_ones_where(gt))
        sel = gt | (eq & (incl_cumsum(eqb) <= need))
        m = _ones_where(sel, BF16)
        glob = incl_cumsum(m)
        tot_row, pref_row = row_prefix(m)
        incl_row = pref_row + tot_row
        _, eq_pref_row = row_prefix(eqb)
        pref_ref[e] = pref_row.astype(I32)
        eqpref_ref[e] = eq_pref_row
        tau_ref[e] = jnp.broadcast_to(lax.bitcast_convert_type(t, F32), (SUBLANES, LANES))
        need_ref[e] = jnp.full((SUBLANES, LANES), need, F32)
        ghi = jnp.floor(glob * (1.0 / MXU_DIM))
        glo = (glob - MXU_DIM * ghi).astype(BF16)
        ghi = ghi.astype(BF16)
        pr = pref_row[0:1, :]
        ir = incl_row[0:1, :]

        n_chunk = cap // LANES
        group = ROUTE_CHUNK_GROUP if n_chunk % ROUTE_CHUNK_GROUP == 0 else 1

        def chunks(cg, carry2):
            cs = [cg * group + k for k in range(group)]
            s_r = [(c * LANES + lax.broadcasted_iota(I32, (LANES, n_rows), 0)).astype(F32) for c in cs]
            onehot = [_ones_where((pr <= s) & (s < ir), BF16) for s in s_r]
            rowid = [jnp.sum(_ones_where(ir <= s), axis=-1, keepdims=True) for s in s_r]
            grow = [MXU_DIM * jnp.dot(oh, ghi, preferred_element_type=F32)
                    + jnp.dot(oh, glo, preferred_element_type=F32) for oh in onehot]
            s_l = [(c * LANES + lax.broadcasted_iota(I32, (LANES, LANES), 0)).astype(F32) for c in cs]
            inrow = [jnp.sum(_ones_where(gr <= s), axis=-1, keepdims=True) for gr, s in zip(grow, s_l)]
            tok = [jnp.minimum(r * LANES + q, n_tok - 1.0) for r, q in zip(rowid, inrow)]
            tok_t = [jnp.broadcast_to(t_, (LANES, LANES)).T for t_ in tok]
            for c, tt in zip(cs, tok_t):
                idx_ref[e, pl.ds(c, 1), :] = tt[0:1, :].astype(I32)
            return carry2

        lax.fori_loop(0, n_chunk // group, chunks, 0)
        return carry

    lax.fori_loop(0, n_exp, per_expert, 0)


def _route(aff3, cap):
    n_exp, n_rows, _ = aff3.shape
    rep = lambda dt, w: jax.ShapeDtypeStruct((n_exp, SUBLANES, w), dt)
    return pl.pallas_call(
        functools.partial(_route_kernel, cap=cap),
        out_shape=[jax.ShapeDtypeStruct((n_exp, cap // LANES, LANES), I32),
                   rep(F32, LANES), rep(F32, LANES), rep(I32, n_rows), rep(F32, n_rows)],
        scratch_shapes=[pltpu.VMEM((n_exp, n_rows, LANES), I32), pltpu.VMEM((n_exp, 1, LANES), I32)],
        compiler_params=pltpu.CompilerParams(vmem_limit_bytes=VMEM_LIMIT),
        name="ec_route",
    )(aff3)


def _ffn_kernel(idx_ref, h2p_hbm, wg_hbm, wu_hbm, wd_hbm, o_ref, xraw_ref, xb_ref, acc_ref, wgb_ref, wub_ref,
                wdb_ref, sem, wsem, *, tc, rows_per_step, slab, fc, nf):
    ct_n = pl.num_programs(1)
    expert = pl.program_id(0)
    tile = expert * ct_n + pl.program_id(1)
    n_tiles = pl.num_programs(0) * ct_n
    slot = tile % 2
    nxt = jnp.minimum(tile + 1, n_tiles - 1)
    w_parity = (tile * nf) % 2

    def row_copy(tile_id, s, dst_slot):
        tok = idx_ref[tile_id * tc + jnp.minimum(s, tc - 1)]
        return pltpu.make_async_copy(
            h2p_hbm.at[pl.ds(pl.multiple_of(tok * slab, slab), slab), :],
            xraw_ref.at[dst_slot, pl.ds(pl.multiple_of(s * slab, slab), slab), :], sem.at[dst_slot])

    def wait_slot(s_):
        pltpu.make_async_copy(xraw_ref.at[s_], xraw_ref.at[s_], sem.at[s_]).wait()

    def weight_copies(e, f, ws):
        cols = pl.ds(pl.multiple_of(f * fc, fc), fc)
        return (pltpu.make_async_copy(wg_hbm.at[e, :, cols], wgb_ref.at[ws], wsem.at[ws]),
                pltpu.make_async_copy(wu_hbm.at[e, :, cols], wub_ref.at[ws], wsem.at[ws]),
                pltpu.make_async_copy(wd_hbm.at[e, cols, :], wdb_ref.at[ws], wsem.at[ws]))

    @pl.when(tile == 0)
    def _():
        def body(s, carry):
            row_copy(tile, s, slot).start()
            return carry
        lax.fori_loop(0, rows_per_step * nf, body, 0)
        for c in weight_copies(expert, 0, w_parity):
            c.start()

    wait_slot(slot)
    half = slab * LANES
    for j in range(slab):
        w = xraw_ref[slot, pl.ds(j, tc, stride=slab), :]
        lo = lax.bitcast_convert_type(lax.shift_left(w, 16), F32)
        hi = lax.bitcast_convert_type(w & jnp.int32(-65536), F32)
        xb_ref[:, j * LANES:(j + 1) * LANES] = lo.astype(BF16)
        xb_ref[:, half + j * LANES:half + (j + 1) * LANES] = hi.astype(BF16)
    def chunk(f, is_first=False, is_last=False):
        ws = (w_parity + f) % 2
        for c in weight_copies(expert, f, ws):
            c.wait()
        if is_last:
            @pl.when(tile < n_tiles - 1)
            def _():
                for c in weight_copies(nxt // ct_n, 0, 1 - ws):
                    c.start()
        else:
            for c in weight_copies(expert, f + 1, 1 - ws):
                c.start()

        for u in range(rows_per_step):
            row_copy(nxt, f * rows_per_step + u, 1 - slot).start()

        x = xb_ref[...]
        g = jnp.dot(x, wgb_ref[ws].astype(BF16), preferred_element_type=F32)
        up = jnp.dot(x, wub_ref[ws].astype(BF16), preferred_element_type=F32)
        hmid = (g * _sigmoid(g) * up).astype(BF16)
        part = jnp.dot(hmid, wdb_ref[ws].astype(BF16), preferred_element_type=F32)
        if is_first:
            acc_ref[...] = part
        elif is_last:
            o_ref[...] = (acc_ref[...] + part).astype(BF16)
        else:
            acc_ref[...] += part

    def middle(f, carry):
        chunk(f)
        return carry

    chunk(0, is_first=True)
    lax.fori_loop(1, nf - 1, middle, 0)
    chunk(nf - 1, is_last=True)

    @pl.when(tile == n_tiles - 1)
    def _():
        wait_slot(1 - slot)


def _expert_ffn(idx_flat, h2p, w_gate, w_up, w_down, cap, tc, fc):
    n_exp, d, d_ff = w_gate.shape
    slab = d // 2 // LANES
    nct = cap // tc
    nf = d_ff // fc
    assert nf >= 2
    rows_per_step = -(-tc // nf)
    any_spec = pl.BlockSpec(memory_space=pl.ANY)
    grid_spec = pltpu.PrefetchScalarGridSpec(
        num_scalar_prefetch=1,
        grid=(n_exp, nct),
        in_specs=[any_spec, any_spec, any_spec, any_spec],
        out_specs=pl.BlockSpec((tc, d), lambda ei, ci, idx: (ei * nct + ci, 0)),
        scratch_shapes=[pltpu.VMEM((2, rows_per_step * nf * slab, LANES), I32),
                        pltpu.VMEM((tc, d), BF16),
                        pltpu.VMEM((tc, d), F32),
                        pltpu.VMEM((2, d, fc), w_gate.dtype),
                        pltpu.VMEM((2, d, fc), w_up.dtype),
                        pltpu.VMEM((2, fc, d), w_down.dtype),
                        pltpu.SemaphoreType.DMA((2,)),
                        pltpu.SemaphoreType.DMA((2,))],
    )
    return pl.pallas_call(
        functools.partial(_ffn_kernel, tc=tc, rows_per_step=rows_per_step, slab=slab, fc=fc, nf=nf),
        grid_spec=grid_spec,
        out_shape=jax.ShapeDtypeStruct((n_exp * cap, d), BF16),
        compiler_params=_cparams(("arbitrary", "arbitrary")),
        name="expert_ffn",
    )(idx_flat, h2p, w_gate, w_up, w_down)


def _combine_kernel(tab_ref, aff_ref, tau_ref, need_ref, eqs_ref, x1_ref, g2_ref, y_hbm, o_ref,
                    ybuf_ref, sem, *, cap, n_tile):
    i = pl.program_id(0)
    tm, n_exp = aff_ref.shape
    slot_rows = COMBINE_SLOT
    k_rows = n_exp * slot_rows
    total_rows = n_exp * cap
    lane_e = lax.broadcasted_iota(I32, (1, n_exp), 1)

    def geometry(tile, e):
        n0 = tab_ref[e * (n_tile + 1) + tile]
        n1 = tab_ref[e * (n_tile + 1) + tile + 1]
        first = e * cap + n0
        aligned = (first // BF16_ROWS) * BF16_ROWS
        return aligned, first - aligned, n1 - n0

    def round_src(aligned, q):
        src = aligned + q * slot_rows
        clamped = jnp.minimum(src, total_rows - slot_rows)
        return clamped, src - clamped

    def issue_round(tile, q, buf):
        for e in range(n_exp):
            aligned, _, _ = geometry(tile, e)
            src, _ = round_src(aligned, q)
            pltpu.make_async_copy(y_hbm.at[pl.ds(pl.multiple_of(src, BF16_ROWS), slot_rows), :],
                                  ybuf_ref.at[buf, pl.ds(e * slot_rows, slot_rows), :], sem.at[buf]).start()

    buf = i % 2

    @pl.when(i == 0)
    def _():
        issue_round(i, 0, buf)

    @pl.when(i + 1 < n_tile)
    def _():
        issue_round(i + 1, 0, 1 - buf)

    a = aff_ref[...]
    tau = tau_ref[...]
    eq = a == tau
    ti = lax.broadcasted_iota(I32, (tm, tm), 0)
    tj = lax.broadcasted_iota(I32, (tm, tm), 1)
    eq_rank = eqs_ref[...] + jnp.dot(_ones_where(tj <= ti, BF16), _ones_where(eq, BF16),
                                     preferred_element_type=F32)
    sel = (a > tau) | (eq & (eq_rank <= need_ref[...]))
    wm = jnp.where(sel, a, 0.0).astype(BF16)
    rank = jnp.dot(_ones_where(tj < ti, BF16), _ones_where(sel, BF16), preferred_element_type=F32)

    n_round = jnp.int32(1)
    for e in range(n_exp):
        _, delta, n_sel = geometry(i, e)
        n_round = jnp.maximum(n_round, (delta + n_sel + slot_rows - 1) // slot_rows)

    spread = _ones_where(lax.broadcasted_iota(I32, (n_exp, k_rows), 1) // slot_rows
                         == lax.broadcasted_iota(I32, (n_exp, k_rows), 0), BF16)
    col_in_slot = (lax.broadcasted_iota(I32, (tm, k_rows), 1) % slot_rows).astype(F32)
    wm_cols = jnp.dot(wm, spread, preferred_element_type=F32)

    def round_sum(q):
        pltpu.make_async_copy(ybuf_ref.at[buf], ybuf_ref.at[buf], sem.at[buf]).wait()
        offset = jnp.zeros((1, n_exp), F32)
        shift = jnp.zeros((1, n_exp), F32)
        for e in range(n_exp):
            aligned, delta, _ = geometry(i, e)
            _, sh = round_src(aligned, q)
            offset = jnp.where(lane_e == e, (delta - q * slot_rows).astype(F32), offset)
            shift = jnp.where(lane_e == e, sh.astype(F32), shift)
        u = rank + offset
        pos = jnp.where(u >= 0, u + shift, -1.0).astype(BF16)
        pos_cols = jnp.dot(pos, spread, preferred_element_type=F32)
        place = jnp.where(pos_cols == col_in_slot, wm_cols, 0.0).astype(BF16)
        return jnp.dot(place, ybuf_ref[buf], preferred_element_type=F32)

    o_ref[...] = x1_ref[...] + g2_ref[...] * round_sum(0)

    def extra_round(q, carry):
        issue_round(i, q, buf)
        o_ref[...] += g2_ref[...] * round_sum(q)
        return carry

    lax.fori_loop(1, n_round, extra_round, 0)


def _combine(tab, aff, tau, need, eqs, x1, mod3, y, cap, seq_len, tm):
    t, d = x1.shape
    n_exp = aff.shape[1]
    n_tile = t // tm
    seq = lambda i, tab_: ((i * tm) // seq_len, 0, 5)
    grid_spec = pltpu.PrefetchScalarGridSpec(
        num_scalar_prefetch=1,
        grid=(n_tile,),
        in_specs=[pl.BlockSpec((tm, n_exp), lambda i, tab_: (i, 0)),
                  pl.BlockSpec((1, n_exp), lambda i, tab_: (0, 0)),
                  pl.BlockSpec((1, n_exp), lambda i, tab_: (0, 0)),
                  pl.BlockSpec((None, 1, n_exp), lambda i, tab_: (i, 0, 0)),
                  pl.BlockSpec((tm, d), lambda i, tab_: (i, 0)),
                  pl.BlockSpec((None, 1, d), seq),
                  pl.BlockSpec(memory_space=pl.ANY)],
        out_specs=pl.BlockSpec((tm, d), lambda i, tab_: (i, 0)),
        scratch_shapes=[pltpu.VMEM((2, n_exp * COMBINE_SLOT, d), BF16),
                        pltpu.SemaphoreType.DMA((2,))],
    )
    return pl.pallas_call(
        functools.partial(_combine_kernel, cap=cap, n_tile=n_tile),
        grid_spec=grid_spec,
        out_shape=jax.ShapeDtypeStruct((t, d), F32),
        compiler_params=_cparams(("arbitrary",)),
        name="ec_combine",
    )(tab, aff, tau, need, eqs, x1, mod3, y)


def _moe(h2p, aff, afft, x1, mod_g, w_gate, w_up, w_down, seq_len):
    t = aff.shape[0]
    n_exp = aff.shape[1]
    cap = CAPACITY_FACTOR * t // n_exp
    tm = COMBINE_TILE
    idx, tau, need, pref, eqpref = _route(afft.reshape(n_exp, t // LANES, LANES), cap)
    y = _expert_ffn(idx.reshape(-1), h2p, w_gate, w_up, w_down, cap, min(1024, cap), 256)
    rows_per_tile = tm // LANES
    tab = jnp.concatenate([pref[:, 0, ::rows_per_tile], jnp.full((n_exp, 1), cap, I32)], axis=1).reshape(-1)
    eqs = eqpref[:, 0, ::rows_per_tile].T.reshape(t // tm, 1, n_exp)
    return _combine(tab, aff, tau[:, 0, 0].reshape(1, n_exp), need[:, 0, 0].reshape(1, n_exp), eqs,
                    x1, mod_g, y, cap, seq_len, tm)


def _trunk(x, mod3, seq_base, prm):
    b, s, d = x.shape
    t = b * s
    x2 = x.reshape(t, d)
    mod_g = lax.slice_in_dim(mod3, seq_base, seq_base + b, axis=0)
    z = _in_projection(x2, mod_g, prm["norm1_g"], prm["w_in"], prm["q_g"], prm["k_g"], s, min(1024, s))
    attn_o = _attention(z, prm["sink"], s)
    x1, h2p, aff, afft = _post_mixer(attn_o, z, x2, mod_g, prm["norm2_g"], prm["ln_g"], prm["ln_b"], prm["ws"],
                                     prm["bs"], prm["w_oa"], prm["w_os"], prm["w_out"], prm["w_r2"], s, 256)
    out = _moe(h2p, aff, afft, x1, mod_g, prm["w_gate"], prm["w_up"], prm["w_down"], s)
    return out.reshape(b, s, d)


def kernel(x_prompt, x_sample, c_prompt, c_sample, w_ada, b_ada, norm1_g, norm2_g, w_in, q_norm_g, k_norm_g,
           attn_sink, sgu_ln_g, sgu_ln_b, sgu_w, sgu_b, w_o_attn, w_o_sgu, w_out, w_router, w_gate, w_up,
           w_down):
    assert w_ada.shape[0] == 1
    bp, bs_ = x_prompt.shape[0], x_sample.shape[0]
    assert bp + bs_ <= N_SEQ_PAD
    d = D_MODEL
    c_pad = jnp.zeros((N_SEQ_PAD, d), F32).at[:bp].set(c_prompt).at[bp:bp + bs_].set(c_sample)
    mod = _modulation(c_pad, w_ada[0], b_ada[0])
    mod3 = mod.reshape(N_SEQ_PAD, 1, 6 * d)

    w_r = w_router[0]
    w_r_hi = w_r.astype(BF16)
    w_r_lo = (w_r - w_r_hi.astype(F32)).astype(BF16)
    pad = ((0, 0), (0, LANES - N_EXPERTS))
    w_r2 = jnp.concatenate([jnp.pad(w_r_hi, pad), jnp.pad(w_r_lo, pad)], axis=1)
    prm = dict(
        norm1_g=norm1_g[0].reshape(1, d), norm2_g=norm2_g[0].reshape(1, d), w_in=w_in[0].astype(BF16),
        q_g=q_norm_g[0].reshape(1, HEAD_DIM), k_g=k_norm_g[0].reshape(1, HEAD_DIM), sink=attn_sink[0],
        ln_g=sgu_ln_g[0].reshape(1, d), ln_b=sgu_ln_b[0].reshape(1, d),
        ws=sgu_w[0].astype(BF16),
        bs=jnp.broadcast_to(sgu_b[0][:, :, None], (N_SGU_GROUPS, BLOCK, BLOCK)),
        w_oa=w_o_attn[0].astype(BF16), w_os=w_o_sgu[0].astype(BF16), w_out=w_out[0].astype(BF16),
        w_r2=w_r2, w_gate=w_gate[0], w_up=w_up[0], w_down=w_down[0])
    y_prompt = _trunk(x_prompt, mod3, 0, prm)
    y_sample = _trunk(x_sample, mod3, bp, prm)
    return (y_prompt, y_sample)
```

```python
import functools

import jax
import jax.numpy as jnp
from jax import lax
from jax.experimental import pallas as pl
from jax.experimental.pallas import tpu as pltpu

F32 = jnp.float32
BF16 = jnp.bfloat16
I32 = jnp.int32

D_MODEL = 2048
HEAD_DIM = 128
N_Q_HEADS = 16
N_KV_HEADS = 4
Q_PER_KV = N_Q_HEADS // N_KV_HEADS
KV_WIDTH = N_KV_HEADS * HEAD_DIM
BLOCK = 128
N_SGU_GROUPS = 16
N_EXPERTS = 16
CAPACITY_FACTOR = 2
D_FF = 2816
NORM_EPS = 1e-6
LOG2E = 1.4426950408889634
IN_WIDTH = 11264
N_SEQ_PAD = 8

LANES = 128
SUBLANES = 8
BF16_ROWS = 16
MXU_DIM = 256

COL_TILE = 1024
Q_TILES = (0, 2)
GELU_TILES = (2, 6)
SIG_TILES = (6, 10)
KV_TILE = 10
N_COL_TILES = IN_WIDTH // COL_TILE

INPROJ_ROW_CHUNK = 512
ATTN_Q_BLOCKS = 2
ROUTE_CHUNK_GROUP = 4
COMBINE_TILE = 256
COMBINE_SLOT = 64

VMEM_LIMIT = 56 * 1024 * 1024


def _cparams(sem):
    return pltpu.CompilerParams(dimension_semantics=sem, vmem_limit_bytes=VMEM_LIMIT)


def _gelu_tanh(x):
    c = 0.7978845608028654
    return 0.5 * x * (1.0 + jnp.tanh(c * (x + 0.044715 * (x * x * x))))


def _sigmoid(x):
    return 1.0 / (1.0 + jnp.exp(-x))


def _ones_where(cond, dtype=F32):
    return jnp.where(cond, 1.0, 0.0).astype(dtype)


def _mod_kernel(c_ref, w_ref, b_ref, o_ref):
    c = c_ref[...]
    s = c * _sigmoid(c)
    o_ref[...] = jnp.dot(s, w_ref[...], precision=lax.Precision.HIGHEST,
                         preferred_element_type=F32) + b_ref[...]


def _modulation(c_pad, w_ada, b_ada):
    n = w_ada.shape[1]
    tn = 1024
    return pl.pallas_call(
        _mod_kernel,
        grid=(n // tn,),
        in_specs=[pl.BlockSpec((N_SEQ_PAD, D_MODEL), lambda j: (0, 0)),
                  pl.BlockSpec((D_MODEL, tn), lambda j: (0, j)),
                  pl.BlockSpec((1, tn), lambda j: (0, j))],
        out_specs=pl.BlockSpec((N_SEQ_PAD, tn), lambda j: (0, j)),
        out_shape=jax.ShapeDtypeStruct((N_SEQ_PAD, n), F32),
        compiler_params=_cparams(("arbitrary",)),
        name="adaln_mod",
    )(c_pad, w_ada, b_ada.reshape(1, n))


def _inproj_kernel(x_ref, sh_ref, sc_ref, g_ref, w_ref, qg_ref, kg_ref, o_ref, h_ref):
    i = pl.program_id(0)
    j = pl.program_id(1)
    slot = i % 2
    n_chunk = x_ref.shape[0] // INPROJ_ROW_CHUNK

    def norm_rows(rows, dst_slot):
        x = x_ref[rows, :]
        ms = jnp.mean(x * x, axis=-1, keepdims=True)
        y = x * lax.rsqrt(ms + NORM_EPS) * g_ref[...]
        h_ref[dst_slot, rows, :] = (y * (1.0 + sc_ref[...]) + sh_ref[...]).astype(BF16)

    @pl.when((i == 0) & (j == 0))
    def _():
        for r in range(n_chunk):
            norm_rows(pl.ds(r * INPROJ_ROW_CHUNK, INPROJ_ROW_CHUNK), slot)

    def head_norm(sub, g):
        ms = jnp.mean(sub * sub, axis=-1, keepdims=True)
        return sub * lax.rsqrt(ms + NORM_EPS) * g

    def chunked(epilogue, side=None):
        for r in range(n_chunk):
            rows = pl.ds(r * INPROJ_ROW_CHUNK, INPROJ_ROW_CHUNK)
            epilogue(rows, jnp.dot(h_ref[slot, rows, :], w_ref[...], preferred_element_type=F32))
            if side is not None:
                side(rows)

    @pl.when(j < Q_TILES[1])
    def _():
        g = qg_ref[...] * (HEAD_DIM ** -0.5 * LOG2E)

        def epilogue(rows, acc):
            for hh in range(COL_TILE // HEAD_DIM):
                cols = slice(hh * HEAD_DIM, (hh + 1) * HEAD_DIM)
                o_ref[rows, cols] = head_norm(acc[:, cols], g).astype(BF16)
        chunked(epilogue)

    @pl.when((j >= GELU_TILES[0]) & (j < GELU_TILES[1]))
    def _():
        def epilogue(rows, acc):
            o_ref[rows, :] = _gelu_tanh(acc).astype(BF16)
        chunked(epilogue)

    @pl.when((j >= SIG_TILES[0]) & (j < SIG_TILES[1]))
    def _():
        def epilogue(rows, acc):
            o_ref[rows, :] = _sigmoid(acc).astype(BF16)
        chunked(epilogue)

    @pl.when(j == KV_TILE)
    def _():
        g = kg_ref[...]

        def epilogue(rows, acc):
            for hh in range(N_KV_HEADS):
                cols = slice(hh * HEAD_DIM, (hh + 1) * HEAD_DIM)
                o_ref[rows, cols] = head_norm(acc[:, cols], g).astype(BF16)
            o_ref[rows, KV_WIDTH:] = acc[:, KV_WIDTH:].astype(BF16)
        chunked(epilogue, side=lambda rows: norm_rows(rows, 1 - slot))


def _in_projection(x2, mod3, norm_g, w_in_b, q_g, k_g, seq_len, tm):
    t = x2.shape[0]
    n_row_tiles = t // tm
    assert KV_TILE == N_COL_TILES - 1
    x_tile = lambda i, j: jnp.minimum(i + jnp.where(j == N_COL_TILES - 1, 1, 0), n_row_tiles - 1)
    seq = lambda i, j: (x_tile(i, j) * tm) // seq_len
    w_tile = lambda j: jnp.where(j < Q_TILES[1], j, jnp.where(j == KV_TILE, Q_TILES[1], j + 1))
    return pl.pallas_call(
        _inproj_kernel,
        grid=(n_row_tiles, N_COL_TILES),
        in_specs=[pl.BlockSpec((tm, D_MODEL), lambda i, j: (x_tile(i, j), 0)),
                  pl.BlockSpec((None, 1, D_MODEL), lambda i, j: (seq(i, j), 0, 0)),
                  pl.BlockSpec((None, 1, D_MODEL), lambda i, j: (seq(i, j), 0, 1)),
                  pl.BlockSpec((1, D_MODEL), lambda i, j: (0, 0)),
                  pl.BlockSpec((D_MODEL, COL_TILE), lambda i, j: (0, w_tile(j))),
                  pl.BlockSpec((1, HEAD_DIM), lambda i, j: (0, 0)),
                  pl.BlockSpec((1, HEAD_DIM), lambda i, j: (0, 0))],
        out_specs=pl.BlockSpec((tm, COL_TILE), lambda i, j: (i, j)),
        out_shape=jax.ShapeDtypeStruct((t, IN_WIDTH), BF16),
        scratch_shapes=[pltpu.VMEM((2, tm, D_MODEL), BF16)],
        compiler_params=_cparams(("arbitrary", "arbitrary")),
        name="norm_inproj",
    )(x2, mod3, mod3, norm_g, w_in_b, q_g, k_g)


def _alibi_slope(head):
    return 2.0 ** (-8.0 * (head + 1) / N_Q_HEADS)


def _attn_kernel(sink_ref, q_ref, kp_ref, kc_ref, kn_ref, vp_ref, vc_ref, vn_ref, o_ref, pen_ref, *,
                 steps_per_seq):
    n = pl.program_id(0)
    pos = n % steps_per_seq
    first = pos == 0
    last = pos == steps_per_seq - 1

    @pl.when(n == 0)
    def _():
        qi = lax.broadcasted_iota(I32, (BLOCK, 3 * BLOCK), 0)
        kj = lax.broadcasted_iota(I32, (BLOCK, 3 * BLOCK), 1)
        absrel = jnp.abs(kj - BLOCK - qi)
        in_window = absrel <= BLOCK
        absrel_f = absrel.astype(F32)
        masks = (in_window, in_window & (kj >= BLOCK), in_window & (kj < 2 * BLOCK))
        for v, valid in enumerate(masks):
            for hd in range(N_Q_HEADS):
                pen_ref[v, hd] = jnp.where(valid, (_alibi_slope(hd) * LOG2E) * absrel_f, jnp.inf)

    variants = []
    for b in range(ATTN_Q_BLOCKS):
        variant = 0
        if b == 0:
            variant = jnp.where(first, 1, variant)
        if b == ATTN_Q_BLOCKS - 1:
            variant = jnp.where(last, 2, variant)
        variants.append(variant)

    nt_dims = (((1,), (1,)), ((), ()))
    for h in range(N_KV_HEADS):
        kv_cols = slice(h * HEAD_DIM, (h + 1) * HEAD_DIM)
        k_parts = [kp_ref[:, kv_cols]] + [kc_ref[c * BLOCK:(c + 1) * BLOCK, kv_cols]
                                          for c in range(ATTN_Q_BLOCKS)] + [kn_ref[:, kv_cols]]
        v_parts = [vp_ref[:, kv_cols]] + [vc_ref[c * BLOCK:(c + 1) * BLOCK, kv_cols]
                                          for c in range(ATTN_Q_BLOCKS)] + [vn_ref[:, kv_cols]]
        heads = [h * Q_PER_KV + g for g in range(Q_PER_KV)]
        cols = [slice(hd * HEAD_DIM, (hd + 1) * HEAD_DIM) for hd in heads]
        sinks = [sink_ref[hd] * LOG2E for hd in heads]
        blocks = range(ATTN_Q_BLOCKS)
        q_rows = [slice(b * BLOCK, (b + 1) * BLOCK) for b in blocks]
        s4 = [lax.dot_general(jnp.concatenate([q_ref[q_rows[b], c] for c in cols], axis=0),
                              jnp.concatenate(k_parts[b:b + 3], axis=0), nt_dims, preferred_element_type=F32)
              for b in blocks]
        chains = [(b, g) for b in blocks for g in range(Q_PER_KV)]
        s = [s4[b][g * BLOCK:(g + 1) * BLOCK] - pen_ref[variants[b], heads[g]] for b, g in chains]
        m = [jnp.maximum(jnp.max(sc, axis=-1, keepdims=True), sinks[g]) for sc, (b, g) in zip(s, chains)]
        p = [jnp.exp2(sc - mc) for sc, mc in zip(s, m)]
        denom = [jnp.sum(pc, axis=-1, keepdims=True) + jnp.exp2(sinks[g] - mc)
                 for pc, mc, (b, g) in zip(p, m, chains)]
        o4 = [jnp.dot(jnp.concatenate([p[b * Q_PER_KV + g].astype(BF16) for g in range(Q_PER_KV)], axis=0),
                      jnp.concatenate(v_parts[b:b + 3], axis=0), preferred_element_type=F32) for b in blocks]
        for (b, g), dc in zip(chains, denom):
            o_ref[q_rows[b], cols[g]] = (o4[b][g * BLOCK:(g + 1) * BLOCK] / dc).astype(BF16)


def _attention(z, sink, seq_len):
    t = z.shape[0]
    nb = t // BLOCK
    qb = ATTN_Q_BLOCKS
    k_col = (IN_WIDTH - 2 * KV_WIDTH) // KV_WIDTH
    v_col = k_col + 1
    halo = lambda col: [pl.BlockSpec((BLOCK, KV_WIDTH), lambda n, s: (jnp.maximum(qb * n - 1, 0), col)),
                        pl.BlockSpec((qb * BLOCK, KV_WIDTH), lambda n, s: (n, col)),
                        pl.BlockSpec((BLOCK, KV_WIDTH), lambda n, s: (jnp.minimum(qb * n + qb, nb - 1), col))]
    grid_spec = pltpu.PrefetchScalarGridSpec(
        num_scalar_prefetch=1,
        grid=(nb // qb,),
        in_specs=[pl.BlockSpec((qb * BLOCK, D_MODEL), lambda n, s: (n, 0))] + halo(k_col) + halo(v_col),
        out_specs=pl.BlockSpec((qb * BLOCK, D_MODEL), lambda n, s: (n, 0)),
        scratch_shapes=[pltpu.VMEM((3, N_Q_HEADS, BLOCK, 3 * BLOCK), F32)],
    )
    assert qb >= 2
    return pl.pallas_call(
        functools.partial(_attn_kernel, steps_per_seq=seq_len // (qb * BLOCK)),
        grid_spec=grid_spec,
        out_shape=jax.ShapeDtypeStruct((t, D_MODEL), BF16),
        compiler_params=_cparams(("arbitrary",)),
        name="window_attn",
    )(sink, z, z, z, z, z, z, z)


def _post_kernel(attn_ref, u_ref, vg_ref, ga_ref, gs_ref, x_ref, g1_ref, sh2_ref, sc2_ref, n2g_ref,
                 lng_ref, lnb_ref, ws_ref, bs_ref, woa_ref, wos_ref, wout_ref, wr_ref,
                 x1_ref, h2p_ref, aff_ref, afft_ref, sgu_ref):
    step = pl.program_id(0)
    tm = x_ref.shape[0]

    def prepare(dst):
        vg = vg_ref[...].astype(F32)
        mu = jnp.mean(vg, axis=-1, keepdims=True)
        cen = vg - mu
        var = jnp.mean(cen * cen, axis=-1, keepdims=True)
        vn = (cen * lax.rsqrt(var + NORM_EPS) * lng_ref[...] + lnb_ref[...]).astype(BF16)
        for c in range(tm // BLOCK):
            rows = slice(c * BLOCK, (c + 1) * BLOCK)
            for g in range(N_SGU_GROUPS):
                cols = slice(g * BLOCK, (g + 1) * BLOCK)
                mixed = jnp.dot(ws_ref[g], vn[rows, cols], preferred_element_type=F32) + bs_ref[g]
                sgu_ref[dst, rows, cols] = (u_ref[rows, cols].astype(F32) * mixed).astype(BF16)

    @pl.when(step == 0)
    def _():
        prepare(0)

    @pl.when(step > 0)
    def _():
        prepare(step % 2)
        _post_finish(attn_ref, ga_ref, gs_ref, x_ref, g1_ref, sh2_ref, sc2_ref, n2g_ref, woa_ref, wos_ref,
                     wout_ref, wr_ref, x1_ref, h2p_ref, aff_ref, afft_ref, sgu_ref.at[(step + 1) % 2])


def _post_finish(attn_ref, ga_ref, gs_ref, x_ref, g1_ref, sh2_ref, sc2_ref, n2g_ref, woa_ref, wos_ref,
                 wout_ref, wr_ref, x1_ref, h2p_ref, aff_ref, afft_ref, sgu_ref):
    tm = x_ref.shape[0]
    a = jnp.dot(attn_ref[...], woa_ref[...], preferred_element_type=F32)
    s = jnp.dot(sgu_ref[...], wos_ref[...], preferred_element_type=F32)
    merged = (ga_ref[...].astype(F32) * a + gs_ref[...].astype(F32) * s).astype(BF16)
    mix = jnp.dot(merged, wout_ref[...], preferred_element_type=F32)
    x1 = x_ref[...] + g1_ref[...] * mix
    x1_ref[...] = x1
    ms = jnp.mean(x1 * x1, axis=-1, keepdims=True)
    h2 = x1 * lax.rsqrt(ms + NORM_EPS) * n2g_ref[...]
    h2 = h2 * (1.0 + sc2_ref[...]) + sh2_ref[...]
    h2b = h2.astype(BF16)
    half = D_MODEL // 2
    n_slab = half // LANES
    lo = lax.shift_right_logical(lax.bitcast_convert_type(h2b[:, :half].astype(F32), I32), 16)
    hi = lax.bitcast_convert_type(h2b[:, half:].astype(F32), I32) & jnp.int32(-65536)
    word = hi | lo
    for j in range(n_slab):
        h2p_ref[pl.ds(j, tm, stride=n_slab), :] = word[:, j * LANES:(j + 1) * LANES]
    h_lo = (h2 - h2b.astype(F32)).astype(BF16)
    r1 = jnp.dot(h2b, wr_ref[...], preferred_element_type=F32)
    r2 = jnp.dot(h_lo, wr_ref[:, :LANES], preferred_element_type=F32)
    logits = r1[:, :LANES] + r1[:, LANES:] + r2
    lane = lax.broadcasted_iota(I32, logits.shape, 1)
    logits = jnp.where(lane < N_EXPERTS, logits, -jnp.inf)
    logits = logits - jnp.max(logits, axis=-1, keepdims=True)
    ex = jnp.exp(logits)
    aff = ex / jnp.sum(ex, axis=-1, keepdims=True)
    aff_ref[...] = aff[:, :N_EXPERTS]
    afft_ref[...] = aff.T[:N_EXPERTS, :]


def _post_mixer(attn_o, z, x2, mod3, norm2_g, ln_g, ln_b, ws, bs, w_oa, w_os, w_out, w_r2, seq_len, tm):
    t = x2.shape[0]
    n_slab = D_MODEL // 2 // LANES
    n_tiles = t // tm
    ahead = lambda s: jnp.minimum(s, n_tiles - 1)
    done = lambda s: jnp.maximum(s - 1, 0)
    seq = lambda s: (done(s) * tm) // seq_len
    const2 = lambda s: (0, 0)
    const3 = lambda s: (0, 0, 0)
    resident = lambda shape, imap: pl.BlockSpec(shape, imap, pipeline_mode=pl.Buffered(1))
    tok = lambda col: pl.BlockSpec((tm, D_MODEL), lambda s: (done(s), col))
    tok_ahead = lambda col: pl.BlockSpec((tm, D_MODEL), lambda s: (ahead(s), col))
    modv = lambda col: pl.BlockSpec((None, 1, D_MODEL), lambda s: (seq(s), 0, col))
    return pl.pallas_call(
        _post_kernel,
        grid=(n_tiles + 1,),
        in_specs=[tok(0),
                  tok_ahead(1), tok_ahead(2),
                  tok(3), tok(4),
                  tok(0),
                  modv(2), modv(3), modv(4),
                  pl.BlockSpec((1, D_MODEL), const2),
                  pl.BlockSpec((1, D_MODEL), const2),
                  pl.BlockSpec((1, D_MODEL), const2),
                  resident((N_SGU_GROUPS, BLOCK, BLOCK), const3),
                  resident((N_SGU_GROUPS, BLOCK, BLOCK), const3),
                  resident((D_MODEL, D_MODEL), const2),
                  resident((D_MODEL, D_MODEL), const2),
                  resident((D_MODEL, D_MODEL), const2),
                  resident((D_MODEL, 2 * LANES), const2)],
        out_specs=[pl.BlockSpec((tm, D_MODEL), lambda s: (done(s), 0)),
                   pl.BlockSpec((tm * n_slab, LANES), lambda s: (done(s), 0)),
                   pl.BlockSpec((tm, N_EXPERTS), lambda s: (done(s), 0)),
                   pl.BlockSpec((N_EXPERTS, tm), lambda s: (0, done(s)))],
        out_shape=[jax.ShapeDtypeStruct((t, D_MODEL), F32),
                   jax.ShapeDtypeStruct((t * n_slab, LANES), I32),
                   jax.ShapeDtypeStruct((t, N_EXPERTS), F32),
                   jax.ShapeDtypeStruct((N_EXPERTS, t), F32)],
        scratch_shapes=[pltpu.VMEM((2, tm, D_MODEL), BF16)],
        compiler_params=_cparams(("arbitrary",)),
        name="post_mixer",
    )(attn_o, z, z, z, z, x2, mod3, mod3, mod3, norm2_g, ln_g, ln_b, ws, bs, w_oa, w_os, w_out, w_r2)


def _route_kernel(aff_ref, idx_ref, tau_ref, need_ref, pref_ref, eqpref_ref, bits_ref, taus_ref, *, cap):
    n_exp, n_rows, _ = aff_ref.shape
    n_tok = n_rows * LANES
    bits_ref[...] = lax.bitcast_convert_type(aff_ref[...], I32)

    def bisect(i, v):
        cand = v | lax.shift_left(jnp.int32(1), 30 - i)
        ge = _ones_where(bits_ref[...] >= cand)
        cnt = jnp.sum(jnp.sum(ge, axis=1, keepdims=True), axis=2, keepdims=True)
        return jnp.where(cnt >= cap, cand, v)

    taus_ref[...] = lax.fori_loop(0, 31, bisect, jnp.zeros((n_exp, 1, LANES), I32))

    li = lax.broadcasted_iota(I32, (LANES, LANES), 0)
    lj = lax.broadcasted_iota(I32, (LANES, LANES), 1)
    upper_incl = _ones_where(li <= lj, BF16)
    ones_sq = jnp.ones((LANES, LANES), BF16)
    ri = lax.broadcasted_iota(I32, (n_rows, n_rows), 0)
    rj = lax.broadcasted_iota(I32, (n_rows, n_rows), 1)
    lower_strict = _ones_where(rj < ri, BF16)
    upper_strict = _ones_where(ri < rj, BF16)
    ones_rows = jnp.ones((SUBLANES, LANES), BF16)
    nt_dims = (((1,), (1,)), ((), ()))

    def incl_cumsum(mb):
        local = jnp.dot(mb, upper_incl, preferred_element_type=F32)
        totb = jnp.dot(mb, ones_sq, preferred_element_type=F32)
        prefc = jnp.dot(lower_strict, totb.astype(BF16), preferred_element_type=F32)
        return local + prefc

    def row_prefix(mb):
        tot_row = lax.dot_general(ones_rows, mb, nt_dims, preferred_element_type=F32)
        pref_row = jnp.dot(tot_row.astype(BF16), upper_strict, preferred_element_type=F32)
        return tot_row, pref_row

    def per_expert(e, carry):
        t = taus_ref[e]
        b = bits_ref[e]
        gt = b > t
        eq = b == t
        eqb = _ones_where(eq, BF16)
        need = cap - jnp.sum(_ones_where(gt))
        sel = gt | (eq & (incl_cumsum(eqb) <= need))
        m = _ones_where(sel, BF16)
        glob = incl_cumsum(m)
        tot_row, pref_row = row_prefix(m)
        incl_row = pref_row + tot_row
        _, eq_pref_row = row_prefix(eqb)
        pref_ref[e] = pref_row.astype(I32)
        eqpref_ref[e] = eq_pref_row
        tau_ref[e] = jnp.broadcast_to(lax.bitcast_convert_type(t, F32), (SUBLANES, LANES))
        need_ref[e] = jnp.full((SUBLANES, LANES), need, F32)
        ghi = jnp.floor(glob * (1.0 / MXU_DIM))
        glo = (glob - MXU_DIM * ghi).astype(BF16)
        ghi = ghi.astype(BF16)
        pr = pref_row[0:1, :]
        ir = incl_row[0:1, :]

        n_chunk = cap // LANES
        group = ROUTE_CHUNK_GROUP if n_chunk % ROUTE_CHUNK_GROUP == 0 else 1

        def chunks(cg, carry2):
            cs = [cg * group + k for k in range(group)]
            s_r = [(c * LANES + lax.broadcasted_iota(I32, (LANES, n_rows), 0)).astype(F32) for c in cs]
            onehot = [_ones_where((pr <= s) & (s < ir), BF16) for s in s_r]
            rowid = [jnp.sum(_ones_where(ir <= s), axis=-1, keepdims=True) for s in s_r]
            grow = [MXU_DIM * jnp.dot(oh, ghi, preferred_element_type=F32)
                    + jnp.dot(oh, glo, preferred_element_type=F32) for oh in onehot]
            s_l = [(c * LANES + lax.broadcasted_iota(I32, (LANES, LANES), 0)).astype(F32) for c in cs]
            inrow = [jnp.sum(_ones_where(gr <= s), axis=-1, keepdims=True) for gr, s in zip(grow, s_l)]
            tok = [jnp.minimum(r * LANES + q, n_tok - 1.0) for r, q in zip(rowid, inrow)]
            tok_t = [jnp.broadcast_to(t_, (LANES, LANES)).T for t_ in tok]
            for c, tt in zip(cs, tok_t):
                idx_ref[e, pl.ds(c, 1), :] = tt[0:1, :].astype(I32)
            return carry2

        lax.fori_loop(0, n_chunk // group, chunks, 0)
        return carry

    lax.fori_loop(0, n_exp, per_expert, 0)


def _route(aff3, cap):
    n_exp, n_rows, _ = aff3.shape
    rep = lambda dt, w: jax.ShapeDtypeStruct((n_exp, SUBLANES, w), dt)
    return pl.pallas_call(
        functools.partial(_route_kernel, cap=cap),
        out_shape=[jax.ShapeDtypeStruct((n_exp, cap // LANES, LANES), I32),
                   rep(F32, LANES), rep(F32, LANES), rep(I32, n_rows), rep(F32, n_rows)],
        scratch_shapes=[pltpu.VMEM((n_exp, n_rows, LANES), I32), pltpu.VMEM((n_exp, 1, LANES), I32)],
        compiler_params=pltpu.CompilerParams(vmem_limit_bytes=VMEM_LIMIT),
        name="ec_route",
    )(aff3)


def _ffn_kernel(idx_ref, h2p_hbm, wg_hbm, wu_hbm, wd_hbm, o_ref, xraw_ref, xb_ref, acc_ref, wgb_ref, wub_ref,
                wdb_ref, sem, wsem, *, tc, rows_per_step, slab, fc, nf):
    ct_n = pl.num_programs(1)
    expert = pl.program_id(0)
    tile = expert * ct_n + pl.program_id(1)
    n_tiles = pl.num_programs(0) * ct_n
    slot = tile % 2
    nxt = jnp.minimum(tile + 1, n_tiles - 1)
    w_parity = (tile * nf) % 2

    def row_copy(tile_id, s, dst_slot):
        tok = idx_ref[tile_id * tc + jnp.minimum(s, tc - 1)]
        return pltpu.make_async_copy(
            h2p_hbm.at[pl.ds(pl.multiple_of(tok * slab, slab), slab), :],
            xraw_ref.at[dst_slot, pl.ds(pl.multiple_of(s * slab, slab), slab), :], sem.at[dst_slot])

    def wait_slot(s_):
        pltpu.make_async_copy(xraw_ref.at[s_], xraw_ref.at[s_], sem.at[s_]).wait()

    def weight_copies(e, f, ws):
        cols = pl.ds(pl.multiple_of(f * fc, fc), fc)
        return (pltpu.make_async_copy(wg_hbm.at[e, :, cols], wgb_ref.at[ws], wsem.at[ws]),
                pltpu.make_async_copy(wu_hbm.at[e, :, cols], wub_ref.at[ws], wsem.at[ws]),
                pltpu.make_async_copy(wd_hbm.at[e, cols, :], wdb_ref.at[ws], wsem.at[ws]))

    @pl.when(tile == 0)
    def _():
        def body(s, carry):
            row_copy(tile, s, slot).start()
            return carry
        lax.fori_loop(0, rows_per_step * nf, body, 0)
        for c in weight_copies(expert, 0, w_parity):
            c.start()

    wait_slot(slot)
    half = slab * LANES
    for j in range(slab):
        w = xraw_ref[slot, pl.ds(j, tc, stride=slab), :]
        lo = lax.bitcast_convert_type(lax.shift_left(w, 16), F32)
        hi = lax.bitcast_convert_type(w & jnp.int32(-65536), F32)
        xb_ref[:, j * LANES:(j + 1) * LANES] = lo.astype(BF16)
        xb_ref[:, half + j * LANES:half + (j + 1) * LANES] = hi.astype(BF16)
    def chunk(f, is_first=False, is_last=False):
        ws = (w_parity + f) % 2
        for c in weight_copies(expert, f, ws):
            c.wait()
        if is_last:
            @pl.when(tile < n_tiles - 1)
            def _():
                for c in weight_copies(nxt // ct_n, 0, 1 - ws):
                    c.start()
        else:
            for c in weight_copies(expert, f + 1, 1 - ws):
                c.start()

        for u in range(rows_per_step):
            row_copy(nxt, f * rows_per_step + u, 1 - slot).start()

        x = xb_ref[...]
        g = jnp.dot(x, wgb_ref[ws].astype(BF16), preferred_element_type=F32)
        up = jnp.dot(x, wub_ref[ws].astype(BF16), preferred_element_type=F32)
        hmid = (g * _sigmoid(g) * up).astype(BF16)
        part = jnp.dot(hmid, wdb_ref[ws].astype(BF16), preferred_element_type=F32)
        if is_first:
            acc_ref[...] = part
        elif is_last:
            o_ref[...] = (acc_ref[...] + part).astype(BF16)
        else:
            acc_ref[...] += part

    def middle(f, carry):
        chunk(f)
        return carry

    chunk(0, is_first=True)
    lax.fori_loop(1, nf - 1, middle, 0)
    chunk(nf - 1, is_last=True)

    @pl.when(tile == n_tiles - 1)
    def _():
        wait_slot(1 - slot)


def _expert_ffn(idx_flat, h2p, w_gate, w_up, w_down, cap, tc, fc):
    n_exp, d, d_ff = w_gate.shape
    slab = d // 2 // LANES
    nct = cap // tc
    nf = d_ff // fc
    assert nf >= 2
    rows_per_step = -(-tc // nf)
    any_spec = pl.BlockSpec(memory_space=pl.ANY)
    grid_spec = pltpu.PrefetchScalarGridSpec(
        num_scalar_prefetch=1,
        grid=(n_exp, nct),
        in_specs=[any_spec, any_spec, any_spec, any_spec],
        out_specs=pl.BlockSpec((tc, d), lambda ei, ci, idx: (ei * nct + ci, 0)),
        scratch_shapes=[pltpu.VMEM((2, rows_per_step * nf * slab, LANES), I32),
                        pltpu.VMEM((tc, d), BF16),
                        pltpu.VMEM((tc, d), F32),
                        pltpu.VMEM((2, d, fc), w_gate.dtype),
                        pltpu.VMEM((2, d, fc), w_up.dtype),
                        pltpu.VMEM((2, fc, d), w_down.dtype),
                        pltpu.SemaphoreType.DMA((2,)),
                        pltpu.SemaphoreType.DMA((2,))],
    )
    return pl.pallas_call(
        functools.partial(_ffn_kernel, tc=tc, rows_per_step=rows_per_step, slab=slab, fc=fc, nf=nf),
        grid_spec=grid_spec,
        out_shape=jax.ShapeDtypeStruct((n_exp * cap, d), BF16),
        compiler_params=_cparams(("arbitrary", "arbitrary")),
        name="expert_ffn",
    )(idx_flat, h2p, w_gate, w_up, w_down)


def _combine_kernel(tab_ref, aff_ref, tau_ref, need_ref, eqs_ref, x1_ref, g2_ref, y_hbm, o_ref,
                    ybuf_ref, sem, *, cap, n_tile):
    i = pl.program_id(0)
    tm, n_exp = aff_ref.shape
    slot_rows = COMBINE_SLOT
    k_rows = n_exp * slot_rows
    total_rows = n_exp * cap
    lane_e = lax.broadcasted_iota(I32, (1, n_exp), 1)

    def geometry(tile, e):
        n0 = tab_ref[e * (n_tile + 1) + tile]
        n1 = tab_ref[e * (n_tile + 1) + tile + 1]
        first = e * cap + n0
        aligned = (first // BF16_ROWS) * BF16_ROWS
        return aligned, first - aligned, n1 - n0

    def round_src(aligned, q):
        src = aligned + q * slot_rows
        clamped = jnp.minimum(src, total_rows - slot_rows)
        return clamped, src - clamped

    def issue_round(tile, q, buf):
        for e in range(n_exp):
            aligned, _, _ = geometry(tile, e)
            src, _ = round_src(aligned, q)
            pltpu.make_async_copy(y_hbm.at[pl.ds(pl.multiple_of(src, BF16_ROWS), slot_rows), :],
                                  ybuf_ref.at[buf, pl.ds(e * slot_rows, slot_rows), :], sem.at[buf]).start()

    buf = i % 2

    @pl.when(i == 0)
    def _():
        issue_round(i, 0, buf)

    @pl.when(i + 1 < n_tile)
    def _():
        issue_round(i + 1, 0, 1 - buf)

    a = aff_ref[...]
    tau = tau_ref[...]
    eq = a == tau
    ti = lax.broadcasted_iota(I32, (tm, tm), 0)
    tj = lax.broadcasted_iota(I32, (tm, tm), 1)
    eq_rank = eqs_ref[...] + jnp.dot(_ones_where(tj <= ti, BF16), _ones_where(eq, BF16),
                                     preferred_element_type=F32)
    sel = (a > tau) | (eq & (eq_rank <= need_ref[...]))
    wm = jnp.where(sel, a, 0.0).astype(BF16)
    rank = jnp.dot(_ones_where(tj < ti, BF16), _ones_where(sel, BF16), preferred_element_type=F32)

    n_round = jnp.int32(1)
    for e in range(n_exp):
        _, delta, n_sel = geometry(i, e)
        n_round = jnp.maximum(n_round, (delta + n_sel + slot_rows - 1) // slot_rows)

    spread = _ones_where(lax.broadcasted_iota(I32, (n_exp, k_rows), 1) // slot_rows
                         == lax.broadcasted_iota(I32, (n_exp, k_rows), 0), BF16)
    col_in_slot = (lax.broadcasted_iota(I32, (tm, k_rows), 1) % slot_rows).astype(F32)
    wm_cols = jnp.dot(wm, spread, preferred_element_type=F32)

    def round_sum(q):
        pltpu.make_async_copy(ybuf_ref.at[buf], ybuf_ref.at[buf], sem.at[buf]).wait()
        offset = jnp.zeros((1, n_exp), F32)
        shift = jnp.zeros((1, n_exp), F32)
        for e in range(n_exp):
            aligned, delta, _ = geometry(i, e)
            _, sh = round_src(aligned, q)
            offset = jnp.where(lane_e == e, (delta - q * slot_rows).astype(F32), offset)
            shift = jnp.where(lane_e == e, sh.astype(F32), shift)
        u = rank + offset
        pos = jnp.where(u >= 0, u + shift, -1.0).astype(BF16)
        pos_cols = jnp.dot(pos, spread, preferred_element_type=F32)
        place = jnp.where(pos_cols == col_in_slot, wm_cols, 0.0).astype(BF16)
        return jnp.dot(place, ybuf_ref[buf], preferred_element_type=F32)

    o_ref[...] = x1_ref[...] + g2_ref[...] * round_sum(0)

    def extra_round(q, carry):
        issue_round(i, q, buf)
        o_ref[...] += g2_ref[...] * round_sum(q)
        return carry

    lax.fori_loop(1, n_round, extra_round, 0)


def _combine(tab, aff, tau, need, eqs, x1, mod3, y, cap, seq_len, tm):
    t, d = x1.shape
    n_exp = aff.shape[1]
    n_tile = t // tm
    seq = lambda i, tab_: ((i * tm) // seq_len, 0, 5)
    grid_spec = pltpu.PrefetchScalarGridSpec(
        num_scalar_prefetch=1,
        grid=(n_tile,),
        in_specs=[pl.BlockSpec((tm, n_exp), lambda i, tab_: (i, 0)),
                  pl.BlockSpec((1, n_exp), lambda i, tab_: (0, 0)),
                  pl.BlockSpec((1, n_exp), lambda i, tab_: (0, 0)),
                  pl.BlockSpec((None, 1, n_exp), lambda i, tab_: (i, 0, 0)),
                  pl.BlockSpec((tm, d), lambda i, tab_: (i, 0)),
                  pl.BlockSpec((None, 1, d), seq),
                  pl.BlockSpec(memory_space=pl.ANY)],
        out_specs=pl.BlockSpec((tm, d), lambda i, tab_: (i, 0)),
        scratch_shapes=[pltpu.VMEM((2, n_exp * COMBINE_SLOT, d), BF16),
                        pltpu.SemaphoreType.DMA((2,))],
    )
    return pl.pallas_call(
        functools.partial(_combine_kernel, cap=cap, n_tile=n_tile),
        grid_spec=grid_spec,
        out_shape=jax.ShapeDtypeStruct((t, d), F32),
        compiler_params=_cparams(("arbitrary",)),
        name="ec_combine",
    )(tab, aff, tau, need, eqs, x1, mod3, y)


def _moe(h2p, aff, afft, x1, mod_g, w_gate, w_up, w_down, seq_len):
    t = aff.shape[0]
    n_exp = aff.shape[1]
    cap = CAPACITY_FACTOR * t // n_exp
    tm = COMBINE_TILE
    idx, tau, need, pref, eqpref = _route(afft.reshape(n_exp, t // LANES, LANES), cap)
    y = _expert_ffn(idx.reshape(-1), h2p, w_gate, w_up, w_down, cap, min(1024, cap), 256)
    rows_per_tile = tm // LANES
    tab = jnp.concatenate([pref[:, 0, ::rows_per_tile], jnp.full((n_exp, 1), cap, I32)], axis=1).reshape(-1)
    eqs = eqpref[:, 0, ::rows_per_tile].T.reshape(t // tm, 1, n_exp)
    return _combine(tab, aff, tau[:, 0, 0].reshape(1, n_exp), need[:, 0, 0].reshape(1, n_exp), eqs,
                    x1, mod_g, y, cap, seq_len, tm)


def _trunk(x, mod3, seq_base, prm):
    b, s, d = x.shape
    t = b * s
    x2 = x.reshape(t, d)
    mod_g = lax.slice_in_dim(mod3, seq_base, seq_base + b, axis=0)
    z = _in_projection(x2, mod_g, prm["norm1_g"], prm["w_in"], prm["q_g"], prm["k_g"], s, min(1024, s))
    attn_o = _attention(z, prm["sink"], s)
    x1, h2p, aff, afft = _post_mixer(attn_o, z, x2, mod_g, prm["norm2_g"], prm["ln_g"], prm["ln_b"], prm["ws"],
                                     prm["bs"], prm["w_oa"], prm["w_os"], prm["w_out"], prm["w_r2"], s, 256)
    out = _moe(h2p, aff, afft, x1, mod_g, prm["w_gate"], prm["w_up"], prm["w_down"], s)
    return out.reshape(b, s, d)


def kernel(x_prompt, x_sample, c_prompt, c_sample, w_ada, b_ada, norm1_g, norm2_g, w_in, q_norm_g, k_norm_g,
           attn_sink, sgu_ln_g, sgu_ln_b, sgu_w, sgu_b, w_o_attn, w_o_sgu, w_out, w_router, w_gate, w_up,
           w_down):
    assert w_ada.shape[0] == 1
    bp, bs_ = x_prompt.shape[0], x_sample.shape[0]
    assert bp + bs_ <= N_SEQ_PAD
    d = D_MODEL
    c_pad = jnp.zeros((N_SEQ_PAD, d), F32).at[:bp].set(c_prompt).at[bp:bp + bs_].set(c_sample)
    mod = _modulation(c_pad, w_ada[0], b_ada[0])
    mod3 = mod.reshape(N_SEQ_PAD, 1, 6 * d)

    w_r = w_router[0]
    w_r_hi = w_r.astype(BF16)
    w_r_lo = (w_r - w_r_hi.astype(F32)).astype(BF16)
    pad = ((0, 0), (0, LANES - N_EXPERTS))
    w_r2 = jnp.concatenate([jnp.pad(w_r_hi, pad), jnp.pad(w_r_lo, pad)], axis=1)
    prm = dict(
        norm1_g=norm1_g[0].reshape(1, d), norm2_g=norm2_g[0].reshape(1, d), w_in=w_in[0].astype(BF16),
        q_g=q_norm_g[0].reshape(1, HEAD_DIM), k_g=k_norm_g[0].reshape(1, HEAD_DIM), sink=attn_sink[0],
        ln_g=sgu_ln_g[0].reshape(1, d), ln_b=sgu_ln_b[0].reshape(1, d),
        ws=sgu_w[0].astype(BF16),
        bs=jnp.broadcast_to(sgu_b[0][:, :, None], (N_SGU_GROUPS, BLOCK, BLOCK)),
        w_oa=w_o_attn[0].astype(BF16), w_os=w_o_sgu[0].astype(BF16), w_out=w_out[0].astype(BF16),
        w_r2=w_r2, w_gate=w_gate[0], w_up=w_up[0], w_down=w_down[0])
    y_prompt = _trunk(x_prompt, mod3, 0, prm)
    y_sample = _trunk(x_sample, mod3, bp, prm)
    return (y_prompt, y_sample)
```

```python
import functools

import jax
import jax.numpy as jnp
from jax import lax
from jax.experimental import pallas as pl
from jax.experimental.pallas import tpu as pltpu

F32 = jnp.float32
BF16 = jnp.bfloat16
I32 = jnp.int32

D_MODEL = 2048
HEAD_DIM = 128
N_Q_HEADS = 16
N_KV_HEADS = 4
Q_PER_KV = N_Q_HEADS // N_KV_HEADS
KV_WIDTH = N_KV_HEADS * HEAD_DIM
BLOCK = 128
N_SGU_GROUPS = 16
N_EXPERTS = 16
CAPACITY_FACTOR = 2
D_FF = 2816
NORM_EPS = 1e-6
LOG2E = 1.4426950408889634
IN_WIDTH = 11264
N_SEQ_PAD = 8

LANES = 128
SUBLANES = 8
BF16_ROWS = 16
MXU_DIM = 256

COL_TILE = 1024
Q_TILES = (0, 2)
GELU_TILES = (2, 6)
SIG_TILES = (6, 10)
KV_TILE = 10
N_COL_TILES = IN_WIDTH // COL_TILE

INPROJ_ROW_SPLIT = (0.5, 0.25, 0.25)
ATTN_Q_BLOCKS = 2
ROUTE_CHUNK_GROUP = 4
COMBINE_TILE = 256
COMBINE_SLOT = 64

VMEM_LIMIT = 56 * 1024 * 1024


def _cparams(sem):
    return pltpu.CompilerParams(dimension_semantics=sem, vmem_limit_bytes=VMEM_LIMIT)


def _gelu_tanh(x):
    c = 0.7978845608028654
    return 0.5 * x * (1.0 + jnp.tanh(c * (x + 0.044715 * (x * x * x))))


def _sigmoid(x):
    return 1.0 / (1.0 + jnp.exp(-x))


def _ones_where(cond, dtype=F32):
    return jnp.where(cond, 1.0, 0.0).astype(dtype)


def _mod_kernel(c_ref, w_ref, b_ref, o_ref):
    c = c_ref[...]
    s = c * _sigmoid(c)
    o_ref[...] = jnp.dot(s, w_ref[...], precision=lax.Precision.HIGHEST,
                         preferred_element_type=F32) + b_ref[...]


def _modulation(c_pad, w_ada, b_ada):
    n = w_ada.shape[1]
    tn = 1024
    return pl.pallas_call(
        _mod_kernel,
        grid=(n // tn,),
        in_specs=[pl.BlockSpec((N_SEQ_PAD, D_MODEL), lambda j: (0, 0)),
                  pl.BlockSpec((D_MODEL, tn), lambda j: (0, j)),
                  pl.BlockSpec((1, tn), lambda j: (0, j))],
        out_specs=pl.BlockSpec((N_SEQ_PAD, tn), lambda j: (0, j)),
        out_shape=jax.ShapeDtypeStruct((N_SEQ_PAD, n), F32),
        compiler_params=_cparams(("arbitrary",)),
        name="adaln_mod",
    )(c_pad, w_ada, b_ada.reshape(1, n))


def _inproj_kernel(x_ref, sh_ref, sc_ref, g_ref, w_ref, qg_ref, kg_ref, o_ref, h_ref):
    i = pl.program_id(0)
    j = pl.program_id(1)
    slot = i % 2
    tm = x_ref.shape[0]
    sizes = [int(tm * frac) for frac in INPROJ_ROW_SPLIT]
    assert sum(sizes) == tm and all(sz % BF16_ROWS == 0 for sz in sizes)
    chunks = [pl.ds(sum(sizes[:r]), sizes[r]) for r in range(len(sizes))]

    def norm_rows(rows, dst_slot):
        x = x_ref[rows, :]
        ms = jnp.mean(x * x, axis=-1, keepdims=True)
        y = x * lax.rsqrt(ms + NORM_EPS) * g_ref[...]
        h_ref[dst_slot, rows, :] = (y * (1.0 + sc_ref[...]) + sh_ref[...]).astype(BF16)

    @pl.when((i == 0) & (j == 0))
    def _():
        for rows in chunks:
            norm_rows(rows, slot)

    def head_norm(sub, g):
        ms = jnp.mean(sub * sub, axis=-1, keepdims=True)
        return sub * lax.rsqrt(ms + NORM_EPS) * g

    def chunked(epilogue, side=None):
        for rows in chunks:
            epilogue(rows, jnp.dot(h_ref[slot, rows, :], w_ref[...], preferred_element_type=F32))
            if side is not None:
                side(rows)

    @pl.when(j < Q_TILES[1])
    def _():
        g = qg_ref[...] * (HEAD_DIM ** -0.5 * LOG2E)

        def epilogue(rows, acc):
            for hh in range(COL_TILE // HEAD_DIM):
                cols = slice(hh * HEAD_DIM, (hh + 1) * HEAD_DIM)
                o_ref[rows, cols] = head_norm(acc[:, cols], g).astype(BF16)
        chunked(epilogue)

    @pl.when((j >= GELU_TILES[0]) & (j < GELU_TILES[1]))
    def _():
        def epilogue(rows, acc):
            o_ref[rows, :] = _gelu_tanh(acc).astype(BF16)
        chunked(epilogue)

    @pl.when((j >= SIG_TILES[0]) & (j < SIG_TILES[1]))
    def _():
        def epilogue(rows, acc):
            o_ref[rows, :] = _sigmoid(acc).astype(BF16)
        chunked(epilogue)

    @pl.when(j == KV_TILE)
    def _():
        g = kg_ref[...]

        def epilogue(rows, acc):
            for hh in range(N_KV_HEADS):
                cols = slice(hh * HEAD_DIM, (hh + 1) * HEAD_DIM)
                o_ref[rows, cols] = head_norm(acc[:, cols], g).astype(BF16)
            o_ref[rows, KV_WIDTH:] = acc[:, KV_WIDTH:].astype(BF16)
        chunked(epilogue, side=lambda rows: norm_rows(rows, 1 - slot))


def _in_projection(x2, mod3, norm_g, w_in_b, q_g, k_g, seq_len, tm):
    t = x2.shape[0]
    n_row_tiles = t // tm
    assert KV_TILE == N_COL_TILES - 1
    x_tile = lambda i, j: jnp.minimum(i + jnp.where(j == N_COL_TILES - 1, 1, 0), n_row_tiles - 1)
    seq = lambda i, j: (x_tile(i, j) * tm) // seq_len
    w_tile = lambda j: jnp.where(j < Q_TILES[1], j, jnp.where(j == KV_TILE, Q_TILES[1], j + 1))
    return pl.pallas_call(
        _inproj_kernel,
        grid=(n_row_tiles, N_COL_TILES),
        in_specs=[pl.BlockSpec((tm, D_MODEL), lambda i, j: (x_tile(i, j), 0)),
                  pl.BlockSpec((None, 1, D_MODEL), lambda i, j: (seq(i, j), 0, 0)),
                  pl.BlockSpec((None, 1, D_MODEL), lambda i, j: (seq(i, j), 0, 1)),
                  pl.BlockSpec((1, D_MODEL), lambda i, j: (0, 0)),
                  pl.BlockSpec((D_MODEL, COL_TILE), lambda i, j: (0, w_tile(j))),
                  pl.BlockSpec((1, HEAD_DIM), lambda i, j: (0, 0)),
                  pl.BlockSpec((1, HEAD_DIM), lambda i, j: (0, 0))],
        out_specs=pl.BlockSpec((tm, COL_TILE), lambda i, j: (i, j)),
        out_shape=jax.ShapeDtypeStruct((t, IN_WIDTH), BF16),
        scratch_shapes=[pltpu.VMEM((2, tm, D_MODEL), BF16)],
        compiler_params=_cparams(("arbitrary", "arbitrary")),
        name="norm_inproj",
    )(x2, mod3, mod3, norm_g, w_in_b, q_g, k_g)


def _alibi_slope(head):
    return 2.0 ** (-8.0 * (head + 1) / N_Q_HEADS)


def _attn_kernel(sink_ref, q_ref, kp_ref, kc_ref, kn_ref, vp_ref, vc_ref, vn_ref, o_ref, pen_ref, *,
                 steps_per_seq):
    n = pl.program_id(0)
    pos = n % steps_per_seq
    first = pos == 0
    last = pos == steps_per_seq - 1

    @pl.when(n == 0)
    def _():
        qi = lax.broadcasted_iota(I32, (BLOCK, 3 * BLOCK), 0)
        kj = lax.broadcasted_iota(I32, (BLOCK, 3 * BLOCK), 1)
        absrel = jnp.abs(kj - BLOCK - qi)
        in_window = absrel <= BLOCK
        absrel_f = absrel.astype(F32)
        masks = (in_window, in_window & (kj >= BLOCK), in_window & (kj < 2 * BLOCK))
        for v, valid in enumerate(masks):
            for hd in range(N_Q_HEADS):
                pen_ref[v, hd] = jnp.where(valid, (_alibi_slope(hd) * LOG2E) * absrel_f, jnp.inf)

    variants = []
    for b in range(ATTN_Q_BLOCKS):
        variant = 0
        if b == 0:
            variant = jnp.where(first, 1, variant)
        if b == ATTN_Q_BLOCKS - 1:
            variant = jnp.where(last, 2, variant)
        variants.append(variant)

    nt_dims = (((1,), (1,)), ((), ()))
    for h in range(N_KV_HEADS):
        kv_cols = slice(h * HEAD_DIM, (h + 1) * HEAD_DIM)
        k_parts = [kp_ref[:, kv_cols]] + [kc_ref[c * BLOCK:(c + 1) * BLOCK, kv_cols]
                                          for c in range(ATTN_Q_BLOCKS)] + [kn_ref[:, kv_cols]]
        v_parts = [vp_ref[:, kv_cols]] + [vc_ref[c * BLOCK:(c + 1) * BLOCK, kv_cols]
                                          for c in range(ATTN_Q_BLOCKS)] + [vn_ref[:, kv_cols]]
        heads = [h * Q_PER_KV + g for g in range(Q_PER_KV)]
        cols = [slice(hd * HEAD_DIM, (hd + 1) * HEAD_DIM) for hd in heads]
        sinks = [sink_ref[hd] * LOG2E for hd in heads]
        blocks = range(ATTN_Q_BLOCKS)
        q_rows = [slice(b * BLOCK, (b + 1) * BLOCK) for b in blocks]
        s4 = [lax.dot_general(jnp.concatenate([q_ref[q_rows[b], c] for c in cols], axis=0),
                              jnp.concatenate(k_parts[b:b + 3], axis=0), nt_dims, preferred_element_type=F32)
              for b in blocks]
        chains = [(b, g) for b in blocks for g in range(Q_PER_KV)]
        s = [s4[b][g * BLOCK:(g + 1) * BLOCK] - pen_ref[variants[b], heads[g]] for b, g in chains]
        m = [jnp.maximum(jnp.max(sc, axis=-1, keepdims=True), sinks[g]) for sc, (b, g) in zip(s, chains)]
        p = [jnp.exp2(sc - mc) for sc, mc in zip(s, m)]
        denom = [jnp.sum(pc, axis=-1, keepdims=True) + jnp.exp2(sinks[g] - mc)
                 for pc, mc, (b, g) in zip(p, m, chains)]
        o4 = [jnp.dot(jnp.concatenate([p[b * Q_PER_KV + g].astype(BF16) for g in range(Q_PER_KV)], axis=0),
                      jnp.concatenate(v_parts[b:b + 3], axis=0), preferred_element_type=F32) for b in blocks]
        for (b, g), dc in zip(chains, denom):
            o_ref[q_rows[b], cols[g]] = (o4[b][g * BLOCK:(g + 1) * BLOCK] / dc).astype(BF16)


def _attention(z, sink, seq_len):
    t = z.shape[0]
    nb = t // BLOCK
    qb = ATTN_Q_BLOCKS
    k_col = (IN_WIDTH - 2 * KV_WIDTH) // KV_WIDTH
    v_col = k_col + 1
    halo = lambda col: [pl.BlockSpec((BLOCK, KV_WIDTH), lambda n, s: (jnp.maximum(qb * n - 1, 0), col)),
                        pl.BlockSpec((qb * BLOCK, KV_WIDTH), lambda n, s: (n, col)),
                        pl.BlockSpec((BLOCK, KV_WIDTH), lambda n, s: (jnp.minimum(qb * n + qb, nb - 1), col))]
    grid_spec = pltpu.PrefetchScalarGridSpec(
        num_scalar_prefetch=1,
        grid=(nb // qb,),
        in_specs=[pl.BlockSpec((qb * BLOCK, D_MODEL), lambda n, s: (n, 0))] + halo(k_col) + halo(v_col),
        out_specs=pl.BlockSpec((qb * BLOCK, D_MODEL), lambda n, s: (n, 0)),
        scratch_shapes=[pltpu.VMEM((3, N_Q_HEADS, BLOCK, 3 * BLOCK), F32)],
    )
    assert qb >= 2
    return pl.pallas_call(
        functools.partial(_attn_kernel, steps_per_seq=seq_len // (qb * BLOCK)),
        grid_spec=grid_spec,
        out_shape=jax.ShapeDtypeStruct((t, D_MODEL), BF16),
        compiler_params=_cparams(("arbitrary",)),
        name="window_attn",
    )(sink, z, z, z, z, z, z, z)


def _post_kernel(attn_ref, u_ref, vg_ref, ga_ref, gs_ref, x_ref, g1_ref, sh2_ref, sc2_ref, n2g_ref,
                 lng_ref, lnb_ref, ws_ref, bs_ref, woa_ref, wos_ref, wout_ref, wr_ref,
                 x1_ref, h2p_ref, aff_ref, afft_ref, sgu_ref):
    step = pl.program_id(0)
    tm = x_ref.shape[0]

    def prepare(dst):
        vg = vg_ref[...].astype(F32)
        mu = jnp.mean(vg, axis=-1, keepdims=True)
        cen = vg - mu
        var = jnp.mean(cen * cen, axis=-1, keepdims=True)
        vn = (cen * lax.rsqrt(var + NORM_EPS) * lng_ref[...] + lnb_ref[...]).astype(BF16)
        for c in range(tm // BLOCK):
            rows = slice(c * BLOCK, (c + 1) * BLOCK)
            for g in range(N_SGU_GROUPS):
                cols = slice(g * BLOCK, (g + 1) * BLOCK)
                mixed = jnp.dot(ws_ref[g], vn[rows, cols], preferred_element_type=F32) + bs_ref[g]
                sgu_ref[dst, rows, cols] = (u_ref[rows, cols].astype(F32) * mixed).astype(BF16)

    @pl.when(step == 0)
    def _():
        prepare(0)

    @pl.when(step > 0)
    def _():
        prepare(step % 2)
        _post_finish(attn_ref, ga_ref, gs_ref, x_ref, g1_ref, sh2_ref, sc2_ref, n2g_ref, woa_ref, wos_ref,
                     wout_ref, wr_ref, x1_ref, h2p_ref, aff_ref, afft_ref, sgu_ref.at[(step + 1) % 2])


def _post_finish(attn_ref, ga_ref, gs_ref, x_ref, g1_ref, sh2_ref, sc2_ref, n2g_ref, woa_ref, wos_ref,
                 wout_ref, wr_ref, x1_ref, h2p_ref, aff_ref, afft_ref, sgu_ref):
    tm = x_ref.shape[0]
    a = jnp.dot(attn_ref[...], woa_ref[...], preferred_element_type=F32)
    s = jnp.dot(sgu_ref[...], wos_ref[...], preferred_element_type=F32)
    merged = (ga_ref[...].astype(F32) * a + gs_ref[...].astype(F32) * s).astype(BF16)
    mix = jnp.dot(merged, wout_ref[...], preferred_element_type=F32)
    x1 = x_ref[...] + g1_ref[...] * mix
    x1_ref[...] = x1
    ms = jnp.mean(x1 * x1, axis=-1, keepdims=True)
    h2 = x1 * lax.rsqrt(ms + NORM_EPS) * n2g_ref[...]
    h2 = h2 * (1.0 + sc2_ref[...]) + sh2_ref[...]
    h2b = h2.astype(BF16)
    half = D_MODEL // 2
    n_slab = half // LANES
    lo = lax.shift_right_logical(lax.bitcast_convert_type(h2b[:, :half].astype(F32), I32), 16)
    hi = lax.bitcast_convert_type(h2b[:, half:].astype(F32), I32) & jnp.int32(-65536)
    word = hi | lo
    for j in range(n_slab):
        h2p_ref[pl.ds(j, tm, stride=n_slab), :] = word[:, j * LANES:(j + 1) * LANES]
    h_lo = (h2 - h2b.astype(F32)).astype(BF16)
    r1 = jnp.dot(h2b, wr_ref[...], preferred_element_type=F32)
    r2 = jnp.dot(h_lo, wr_ref[:, :LANES], preferred_element_type=F32)
    logits = r1[:, :LANES] + r1[:, LANES:] + r2
    lane = lax.broadcasted_iota(I32, logits.shape, 1)
    logits = jnp.where(lane < N_EXPERTS, logits, -jnp.inf)
    logits = logits - jnp.max(logits, axis=-1, keepdims=True)
    ex = jnp.exp(logits)
    aff = ex / jnp.sum(ex, axis=-1, keepdims=True)
    aff_ref[...] = aff[:, :N_EXPERTS]
    afft_ref[...] = aff.T[:N_EXPERTS, :]


def _post_mixer(attn_o, z, x2, mod3, norm2_g, ln_g, ln_b, ws, bs, w_oa, w_os, w_out, w_r2, seq_len, tm):
    t = x2.shape[0]
    n_slab = D_MODEL // 2 // LANES
    n_tiles = t // tm
    ahead = lambda s: jnp.minimum(s, n_tiles - 1)
    done = lambda s: jnp.maximum(s - 1, 0)
    seq = lambda s: (done(s) * tm) // seq_len
    const2 = lambda s: (0, 0)
    const3 = lambda s: (0, 0, 0)
    resident = lambda shape, imap: pl.BlockSpec(shape, imap, pipeline_mode=pl.Buffered(1))
    tok = lambda col: pl.BlockSpec((tm, D_MODEL), lambda s: (done(s), col))
    tok_ahead = lambda col: pl.BlockSpec((tm, D_MODEL), lambda s: (ahead(s), col))
    modv = lambda col: pl.BlockSpec((None, 1, D_MODEL), lambda s: (seq(s), 0, col))
    return pl.pallas_call(
        _post_kernel,
        grid=(n_tiles + 1,),
        in_specs=[tok(0),
                  tok_ahead(1), tok_ahead(2),
                  tok(3), tok(4),
                  tok(0),
                  modv(2), modv(3), modv(4),
                  pl.BlockSpec((1, D_MODEL), const2),
                  pl.BlockSpec((1, D_MODEL), const2),
                  pl.BlockSpec((1, D_MODEL), const2),
                  resident((N_SGU_GROUPS, BLOCK, BLOCK), const3),
                  resident((N_SGU_GROUPS, BLOCK, BLOCK), const3),
                  resident((D_MODEL, D_MODEL), const2),
                  resident((D_MODEL, D_MODEL), const2),
                  resident((D_MODEL, D_MODEL), const2),
                  resident((D_MODEL, 2 * LANES), const2)],
        out_specs=[pl.BlockSpec((tm, D_MODEL), lambda s: (done(s), 0)),
                   pl.BlockSpec((tm * n_slab, LANES), lambda s: (done(s), 0)),
                   pl.BlockSpec((tm, N_EXPERTS), lambda s: (done(s), 0)),
                   pl.BlockSpec((N_EXPERTS, tm), lambda s: (0, done(s)))],
        out_shape=[jax.ShapeDtypeStruct((t, D_MODEL), F32),
                   jax.ShapeDtypeStruct((t * n_slab, LANES), I32),
                   jax.ShapeDtypeStruct((t, N_EXPERTS), F32),
                   jax.ShapeDtypeStruct((N_EXPERTS, t), F32)],
        scratch_shapes=[pltpu.VMEM((2, tm, D_MODEL), BF16)],
        compiler_params=_cparams(("arbitrary",)),
        name="post_mixer",
    )(attn_o, z, z, z, z, x2, mod3, mod3, mod3, norm2_g, ln_g, ln_b, ws, bs, w_oa, w_os, w_out, w_r2)


def _route_kernel(aff_ref, idx_ref, tau_ref, need_ref, pref_ref, eqpref_ref, bits_ref, taus_ref, *, cap):
    n_exp, n_rows, _ = aff_ref.shape
    n_tok = n_rows * LANES
    bits_ref[...] = lax.bitcast_convert_type(aff_ref[...], I32)

    def bisect(i, v):
        cand = v | lax.shift_left(jnp.int32(1), 30 - i)
        ge = _ones_where(bits_ref[...] >= cand)
        cnt = jnp.sum(jnp.sum(ge, axis=1, keepdims=True), axis=2, keepdims=True)
        return jnp.where(cnt >= cap, cand, v)

    taus_ref[...] = lax.fori_loop(0, 31, bisect, jnp.zeros((n_exp, 1, LANES), I32))

    li = lax.broadcasted_iota(I32, (LANES, LANES), 0)
    lj = lax.broadcasted_iota(I32, (LANES, LANES), 1)
    upper_incl = _ones_where(li <= lj, BF16)
    ones_sq = jnp.ones((LANES, LANES), BF16)
    ri = lax.broadcasted_iota(I32, (n_rows, n_rows), 0)
    rj = lax.broadcasted_iota(I32, (n_rows, n_rows), 1)
    lower_strict = _ones_where(rj < ri, BF16)
    upper_strict = _ones_where(ri < rj, BF16)
    ones_rows = jnp.ones((SUBLANES, LANES), BF16)
    nt_dims = (((1,), (1,)), ((), ()))

    def incl_cumsum(mb):
        local = jnp.dot(mb, upper_incl, preferred_element_type=F32)
        totb = jnp.dot(mb, ones_sq, preferred_element_type=F32)
        prefc = jnp.dot(lower_strict, totb.astype(BF16), preferred_element_type=F32)
        return local + prefc

    def row_prefix(mb):
        tot_row = lax.dot_general(ones_rows, mb, nt_dims, preferred_element_type=F32)
        pref_row = jnp.dot(tot_row.astype(BF16), upper_strict, preferred_element_type=F32)
        return tot_row, pref_row

    def per_expert(e, carry):
        t = taus_ref[e]
        b = bits_ref[e]
        gt = b > t
        eq = b == t
        eqb = _ones_where(eq, BF16)
        need = cap - jnp.sum(_ones_where(gt))
        sel = gt | (eq & (incl_cumsum(eqb) <= need))
        m = _ones_where(sel, BF16)
        glob = incl_cumsum(m)
        tot_row, pref_row = row_prefix(m)
        incl_row = pref_row + tot_row
        _, eq_pref_row = row_prefix(eqb)
        pref_ref[e] = pref_row.astype(I32)
        eqpref_ref[e] = eq_pref_row
        tau_ref[e] = jnp.broadcast_to(lax.bitcast_convert_type(t, F32), (SUBLANES, LANES))
        need_ref[e] = jnp.full((SUBLANES, LANES), need, F32)
        ghi = jnp.floor(glob * (1.0 / MXU_DIM))
        glo = (glob - MXU_DIM * ghi).astype(BF16)
        ghi = ghi.astype(BF16)
        pr = pref_row[0:1, :]
        ir = incl_row[0:1, :]

        n_chunk = cap // LANES
        group = ROUTE_CHUNK_GROUP if n_chunk % ROUTE_CHUNK_GROUP == 0 else 1

        def chunks(cg, carry2):
            cs = [cg * group + k for k in range(group)]
            s_r = [(c * LANES + lax.broadcasted_iota(I32, (LANES, n_rows), 0)).astype(F32) for c in cs]
            onehot = [_ones_where((pr <= s) & (s < ir), BF16) for s in s_r]
            rowid = [jnp.sum(_ones_where(ir <= s), axis=-1, keepdims=True) for s in s_r]
            grow = [MXU_DIM * jnp.dot(oh, ghi, preferred_element_type=F32)
                    + jnp.dot(oh, glo, preferred_element_type=F32) for oh in onehot]
            s_l = [(c * LANES + lax.broadcasted_iota(I32, (LANES, LANES), 0)).astype(F32) for c in cs]
            inrow = [jnp.sum(_ones_where(gr <= s), axis=-1, keepdims=True) for gr, s in zip(grow, s_l)]
            tok = [jnp.minimum(r * LANES + q, n_tok - 1.0) for r, q in zip(rowid, inrow)]
            tok_t = [jnp.broadcast_to(t_, (LANES, LANES)).T for t_ in tok]
            for c, tt in zip(cs, tok_t):
                idx_ref[e, pl.ds(c, 1), :] = tt[0:1, :].astype(I32)
            return carry2

        lax.fori_loop(0, n_chunk // group, chunks, 0)
        return carry

    lax.fori_loop(0, n_exp, per_expert, 0)


def _route(aff3, cap):
    n_exp, n_rows, _ = aff3.shape
    rep = lambda dt, w: jax.ShapeDtypeStruct((n_exp, SUBLANES, w), dt)
    return pl.pallas_call(
        functools.partial(_route_kernel, cap=cap),
        out_shape=[jax.ShapeDtypeStruct((n_exp, cap // LANES, LANES), I32),
                   rep(F32, LANES), rep(F32, LANES), rep(I32, n_rows), rep(F32, n_rows)],
        scratch_shapes=[pltpu.VMEM((n_exp, n_rows, LANES), I32), pltpu.VMEM((n_exp, 1, LANES), I32)],
        compiler_params=pltpu.CompilerParams(vmem_limit_bytes=VMEM_LIMIT),
        name="ec_route",
    )(aff3)


def _ffn_kernel(idx_ref, h2p_hbm, wg_hbm, wu_hbm, wd_hbm, o_ref, xraw_ref, xb_ref, acc_ref, wgb_ref, wub_ref,
                wdb_ref, sem, wsem, *, tc, rows_per_step, slab, fc, nf):
    ct_n = pl.num_programs(1)
    expert = pl.program_id(0)
    tile = expert * ct_n + pl.program_id(1)
    n_tiles = pl.num_programs(0) * ct_n
    slot = tile % 2
    nxt = jnp.minimum(tile + 1, n_tiles - 1)
    w_parity = (tile * nf) % 2

    def row_copy(tile_id, s, dst_slot):
        tok = idx_ref[tile_id * tc + jnp.minimum(s, tc - 1)]
        return pltpu.make_async_copy(
            h2p_hbm.at[pl.ds(pl.multiple_of(tok * slab, slab), slab), :],
            xraw_ref.at[dst_slot, pl.ds(pl.multiple_of(s * slab, slab), slab), :], sem.at[dst_slot])

    def wait_slot(s_):
        pltpu.make_async_copy(xraw_ref.at[s_], xraw_ref.at[s_], sem.at[s_]).wait()

    def weight_copies(e, f, ws):
        cols = pl.ds(pl.multiple_of(f * fc, fc), fc)
        return (pltpu.make_async_copy(wg_hbm.at[e, :, cols], wgb_ref.at[ws], wsem.at[ws]),
                pltpu.make_async_copy(wu_hbm.at[e, :, cols], wub_ref.at[ws], wsem.at[ws]),
                pltpu.make_async_copy(wd_hbm.at[e, cols, :], wdb_ref.at[ws], wsem.at[ws]))

    @pl.when(tile == 0)
    def _():
        def body(s, carry):
            row_copy(tile, s, slot).start()
            return carry
        lax.fori_loop(0, rows_per_step * nf, body, 0)
        for c in weight_copies(expert, 0, w_parity):
            c.start()

    wait_slot(slot)
    half = slab * LANES
    for j in range(slab):
        w = xraw_ref[slot, pl.ds(j, tc, stride=slab), :]
        lo = lax.bitcast_convert_type(lax.shift_left(w, 16), F32)
        hi = lax.bitcast_convert_type(w & jnp.int32(-65536), F32)
        xb_ref[:, j * LANES:(j + 1) * LANES] = lo.astype(BF16)
        xb_ref[:, half + j * LANES:half + (j + 1) * LANES] = hi.astype(BF16)
    def chunk(f, is_first=False, is_last=False):
        ws = (w_parity + f) % 2
        for c in weight_copies(expert, f, ws):
            c.wait()
        if is_last:
            @pl.when(tile < n_tiles - 1)
            def _():
                for c in weight_copies(nxt // ct_n, 0, 1 - ws):
                    c.start()
        else:
            for c in weight_copies(expert, f + 1, 1 - ws):
                c.start()

        for u in range(rows_per_step):
            row_copy(nxt, f * rows_per_step + u, 1 - slot).start()

        x = xb_ref[...]
        g = jnp.dot(x, wgb_ref[ws].astype(BF16), preferred_element_type=F32)
        up = jnp.dot(x, wub_ref[ws].astype(BF16), preferred_element_type=F32)
        hmid = (g * _sigmoid(g) * up).astype(BF16)
        part = jnp.dot(hmid, wdb_ref[ws].astype(BF16), preferred_element_type=F32)
        if is_first:
            acc_ref[...] = part
        elif is_last:
            o_ref[...] = (acc_ref[...] + part).astype(BF16)
        else:
            acc_ref[...] += part

    chunk(0, is_first=True)
    for f in range(1, nf - 1):
        chunk(f)
    chunk(nf - 1, is_last=True)

    @pl.when(tile == n_tiles - 1)
    def _():
        wait_slot(1 - slot)


def _expert_ffn(idx_flat, h2p, w_gate, w_up, w_down, cap, tc, fc):
    n_exp, d, d_ff = w_gate.shape
    slab = d // 2 // LANES
    nct = cap // tc
    nf = d_ff // fc
    assert nf >= 2
    rows_per_step = -(-tc // nf)
    any_spec = pl.BlockSpec(memory_space=pl.ANY)
    grid_spec = pltpu.PrefetchScalarGridSpec(
        num_scalar_prefetch=1,
        grid=(n_exp, nct),
        in_specs=[any_spec, any_spec, any_spec, any_spec],
        out_specs=pl.BlockSpec((tc, d), lambda ei, ci, idx: (ei * nct + ci, 0)),
        scratch_shapes=[pltpu.VMEM((2, rows_per_step * nf * slab, LANES), I32),
                        pltpu.VMEM((tc, d), BF16),
                        pltpu.VMEM((tc, d), F32),
                        pltpu.VMEM((2, d, fc), w_gate.dtype),
                        pltpu.VMEM((2, d, fc), w_up.dtype),
                        pltpu.VMEM((2, fc, d), w_down.dtype),
                        pltpu.SemaphoreType.DMA((2,)),
                        pltpu.SemaphoreType.DMA((2,))],
    )
    return pl.pallas_call(
        functools.partial(_ffn_kernel, tc=tc, rows_per_step=rows_per_step, slab=slab, fc=fc, nf=nf),
        grid_spec=grid_spec,
        out_shape=jax.ShapeDtypeStruct((n_exp * cap, d), BF16),
        compiler_params=_cparams(("arbitrary", "arbitrary")),
        name="expert_ffn",
    )(idx_flat, h2p, w_gate, w_up, w_down)


def _combine_kernel(tab_ref, aff_ref, tau_ref, need_ref, eqs_ref, x1_ref, g2_ref, y_hbm, o_ref,
                    ybuf_ref, sem, *, cap, n_tile):
    i = pl.program_id(0)
    tm, n_exp = aff_ref.shape
    slot_rows = COMBINE_SLOT
    k_rows = n_exp * slot_rows
    total_rows = n_exp * cap
    lane_e = lax.broadcasted_iota(I32, (1, n_exp), 1)

    def geometry(tile, e):
        n0 = tab_ref[e * (n_tile + 1) + tile]
        n1 = tab_ref[e * (n_tile + 1) + tile + 1]
        first = e * cap + n0
        aligned = (first // BF16_ROWS) * BF16_ROWS
        return aligned, first - aligned, n1 - n0

    def round_src(aligned, q):
        src = aligned + q * slot_rows
        clamped = jnp.minimum(src, total_rows - slot_rows)
        return clamped, src - clamped

    def issue_round(tile, q, buf):
        for e in range(n_exp):
            aligned, _, _ = geometry(tile, e)
            src, _ = round_src(aligned, q)
            pltpu.make_async_copy(y_hbm.at[pl.ds(pl.multiple_of(src, BF16_ROWS), slot_rows), :],
                                  ybuf_ref.at[buf, pl.ds(e * slot_rows, slot_rows), :], sem.at[buf]).start()

    buf = i % 2

    @pl.when(i == 0)
    def _():
        issue_round(i, 0, buf)

    @pl.when(i + 1 < n_tile)
    def _():
        issue_round(i + 1, 0, 1 - buf)

    a = aff_ref[...]
    tau = tau_ref[...]
    eq = a == tau
    ti = lax.broadcasted_iota(I32, (tm, tm), 0)
    tj = lax.broadcasted_iota(I32, (tm, tm), 1)
    eq_rank = eqs_ref[...] + jnp.dot(_ones_where(tj <= ti, BF16), _ones_where(eq, BF16),
                                     preferred_element_type=F32)
    sel = (a > tau) | (eq & (eq_rank <= need_ref[...]))
    wm = jnp.where(sel, a, 0.0).astype(BF16)
    rank = jnp.dot(_ones_where(tj < ti, BF16), _ones_where(sel, BF16), preferred_element_type=F32)

    n_round = jnp.int32(1)
    for e in range(n_exp):
        _, delta, n_sel = geometry(i, e)
        n_round = jnp.maximum(n_round, (delta + n_sel + slot_rows - 1) // slot_rows)

    spread = _ones_where(lax.broadcasted_iota(I32, (n_exp, k_rows), 1) // slot_rows
                         == lax.broadcasted_iota(I32, (n_exp, k_rows), 0), BF16)
    col_in_slot = (lax.broadcasted_iota(I32, (tm, k_rows), 1) % slot_rows).astype(F32)
    wm_cols = jnp.dot(wm, spread, preferred_element_type=F32)

    def round_sum(q):
        pltpu.make_async_copy(ybuf_ref.at[buf], ybuf_ref.at[buf], sem.at[buf]).wait()
        offset = jnp.zeros((1, n_exp), F32)
        shift = jnp.zeros((1, n_exp), F32)
        for e in range(n_exp):
            aligned, delta, _ = geometry(i, e)
            _, sh = round_src(aligned, q)
            offset = jnp.where(lane_e == e, (delta - q * slot_rows).astype(F32), offset)
            shift = jnp.where(lane_e == e, sh.astype(F32), shift)
        u = rank + offset
        pos = jnp.where(u >= 0, u + shift, -1.0).astype(BF16)
        pos_cols = jnp.dot(pos, spread, preferred_element_type=F32)
        place = jnp.where(pos_cols == col_in_slot, wm_cols, 0.0).astype(BF16)
        return jnp.dot(place, ybuf_ref[buf], preferred_element_type=F32)

    o_ref[...] = x1_ref[...] + g2_ref[...] * round_sum(0)

    def extra_round(q, carry):
        issue_round(i, q, buf)
        o_ref[...] += g2_ref[...] * round_sum(q)
        return carry

    lax.fori_loop(1, n_round, extra_round, 0)


def _combine(tab, aff, tau, need, eqs, x1, mod3, y, cap, seq_len, tm):
    t, d = x1.shape
    n_exp = aff.shape[1]
    n_tile = t // tm
    seq = lambda i, tab_: ((i * tm) // seq_len, 0, 5)
    grid_spec = pltpu.PrefetchScalarGridSpec(
        num_scalar_prefetch=1,
        grid=(n_tile,),
        in_specs=[pl.BlockSpec((tm, n_exp), lambda i, tab_: (i, 0)),
                  pl.BlockSpec((1, n_exp), lambda i, tab_: (0, 0)),
                  pl.BlockSpec((1, n_exp), lambda i, tab_: (0, 0)),
                  pl.BlockSpec((None, 1, n_exp), lambda i, tab_: (i, 0, 0)),
                  pl.BlockSpec((tm, d), lambda i, tab_: (i, 0)),
                  pl.BlockSpec((None, 1, d), seq),
                  pl.BlockSpec(memory_space=pl.ANY)],
        out_specs=pl.BlockSpec((tm, d), lambda i, tab_: (i, 0)),
        scratch_shapes=[pltpu.VMEM((2, n_exp * COMBINE_SLOT, d), BF16),
                        pltpu.SemaphoreType.DMA((2,))],
    )
    return pl.pallas_call(
        functools.partial(_combine_kernel, cap=cap, n_tile=n_tile),
        grid_spec=grid_spec,
        out_shape=jax.ShapeDtypeStruct((t, d), F32),
        compiler_params=_cparams(("arbitrary",)),
        name="ec_combine",
    )(tab, aff, tau, need, eqs, x1, mod3, y)


def _moe(h2p, aff, afft, x1, mod_g, w_gate, w_up, w_down, seq_len):
    t = aff.shape[0]
    n_exp = aff.shape[1]
    cap = CAPACITY_FACTOR * t // n_exp
    tm = COMBINE_TILE
    idx, tau, need, pref, eqpref = _route(afft.reshape(n_exp, t // LANES, LANES), cap)
    y = _expert_ffn(idx.reshape(-1), h2p, w_gate, w_up, w_down, cap, min(1024, cap), 256)
    rows_per_tile = tm // LANES
    tab = jnp.concatenate([pref[:, 0, ::rows_per_tile], jnp.full((n_exp, 1), cap, I32)], axis=1).reshape(-1)
    eqs = eqpref[:, 0, ::rows_per_tile].T.reshape(t // tm, 1, n_exp)
    return _combine(tab, aff, tau[:, 0, 0].reshape(1, n_exp), need[:, 0, 0].reshape(1, n_exp), eqs,
                    x1, mod_g, y, cap, seq_len, tm)


def _trunk(x, mod3, seq_base, prm):
    b, s, d = x.shape
    t = b * s
    x2 = x.reshape(t, d)
    mod_g = lax.slice_in_dim(mod3, seq_base, seq_base + b, axis=0)
    z = _in_projection(x2, mod_g, prm["norm1_g"], prm["w_in"], prm["q_g"], prm["k_g"], s, min(1024, s))
    attn_o = _attention(z, prm["sink"], s)
    x1, h2p, aff, afft = _post_mixer(attn_o, z, x2, mod_g, prm["norm2_g"], prm["ln_g"], prm["ln_b"], prm["ws"],
                                     prm["bs"], prm["w_oa"], prm["w_os"], prm["w_out"], prm["w_r2"], s, 256)
    out = _moe(h2p, aff, afft, x1, mod_g, prm["w_gate"], prm["w_up"], prm["w_down"], s)
    return out.reshape(b, s, d)


def kernel(x_prompt, x_sample, c_prompt, c_sample, w_ada, b_ada, norm1_g, norm2_g, w_in, q_norm_g, k_norm_g,
           attn_sink, sgu_ln_g, sgu_ln_b, sgu_w, sgu_b, w_o_attn, w_o_sgu, w_out, w_router, w_gate, w_up,
           w_down):
    assert w_ada.shape[0] == 1
    bp, bs_ = x_prompt.shape[0], x_sample.shape[0]
    assert bp + bs_ <= N_SEQ_PAD
    d = D_MODEL
    c_pad = jnp.zeros((N_SEQ_PAD, d), F32).at[:bp].set(c_prompt).at[bp:bp + bs_].set(c_sample)
    mod = _modulation(c_pad, w_ada[0], b_ada[0])
    mod3 = mod.reshape(N_SEQ_PAD, 1, 6 * d)

    w_r = w_router[0]
    w_r_hi = w_r.astype(BF16)
    w_r_lo = (w_r - w_r_hi.astype(F32)).astype(BF16)
    pad = ((0, 0), (0, LANES - N_EXPERTS))
    w_r2 = jnp.concatenate([jnp.pad(w_r_hi, pad), jnp.pad(w_r_lo, pad)], axis=1)
    prm = dict(
        norm1_g=norm1_g[0].reshape(1, d), norm2_g=norm2_g[0].reshape(1, d), w_in=w_in[0].astype(BF16),
        q_g=q_norm_g[0].reshape(1, HEAD_DIM), k_g=k_norm_g[0].reshape(1, HEAD_DIM), sink=attn_sink[0],
        ln_g=sgu_ln_g[0].reshape(1, d), ln_b=sgu_ln_b[0].reshape(1, d),
        ws=sgu_w[0].astype(BF16),
        bs=jnp.broadcast_to(sgu_b[0][:, :, None], (N_SGU_GROUPS, BLOCK, BLOCK)),
        w_oa=w_o_attn[0].astype(BF16), w_os=w_o_sgu[0].astype(BF16), w_out=w_out[0].astype(BF16),
        w_r2=w_r2, w_gate=w_gate[0], w_up=w_up[0], w_down=w_down[0])
    y_prompt = _trunk(x_prompt, mod3, 0, prm)
    y_sample = _trunk(x_sample, mod3, bp, prm)
    return (y_prompt, y_sample)
```

```python
import functools

import jax
import jax.numpy as jnp
from jax import lax
from jax.experimental import pallas as pl
from jax.experimental.pallas import tpu as pltpu

F32 = jnp.float32
BF16 = jnp.bfloat16
I32 = jnp.int32

D_MODEL = 2048
HEAD_DIM = 128
N_Q_HEADS = 16
N_KV_HEADS = 4
Q_PER_KV = N_Q_HEADS // N_KV_HEADS
KV_WIDTH = N_KV_HEADS * HEAD_DIM
BLOCK = 128
N_SGU_GROUPS = 16
N_EXPERTS = 16
CAPACITY_FACTOR = 2
D_FF = 2816
NORM_EPS = 1e-6
LOG2E = 1.4426950408889634
IN_WIDTH = 11264
N_SEQ_PAD = 8

LANES = 128
SUBLANES = 8
BF16_ROWS = 16
MXU_DIM = 256

COL_TILE = 1024
Q_TILES = (0, 2)
GELU_TILES = (2, 6)
SIG_TILES = (6, 10)
KV_TILE = 10
N_COL_TILES = IN_WIDTH // COL_TILE

INPROJ_ROW_CHUNK = 512
ATTN_Q_BLOCKS = 2
ROUTE_CHUNK_GROUP = 4
COMBINE_TILE = 256
COMBINE_SLOT = 64

VMEM_LIMIT = 56 * 1024 * 1024


def _cparams(sem):
    return pltpu.CompilerParams(dimension_semantics=sem, vmem_limit_bytes=VMEM_LIMIT)


def _gelu_tanh(x):
    c = 0.7978845608028654
    half = 0.5 * x
    return half + half * jnp.tanh(x * (c + (c * 0.044715) * (x * x)))


def _sigmoid(x):
    return 0.5 + 0.5 * jnp.tanh(0.5 * x)


def _ones_where(cond, dtype=F32):
    return jnp.where(cond, 1.0, 0.0).astype(dtype)


def _mod_kernel(c_ref, w_ref, b_ref, o_ref):
    c = c_ref[...]
    s = c * _sigmoid(c)
    o_ref[...] = jnp.dot(s, w_ref[...], precision=lax.Precision.HIGHEST,
                         preferred_element_type=F32) + b_ref[...]


def _modulation(c_pad, w_ada, b_ada):
    n = w_ada.shape[1]
    tn = 1024
    return pl.pallas_call(
        _mod_kernel,
        grid=(n // tn,),
        in_specs=[pl.BlockSpec((N_SEQ_PAD, D_MODEL), lambda j: (0, 0)),
                  pl.BlockSpec((D_MODEL, tn), lambda j: (0, j)),
                  pl.BlockSpec((1, tn), lambda j: (0, j))],
        out_specs=pl.BlockSpec((N_SEQ_PAD, tn), lambda j: (0, j)),
        out_shape=jax.ShapeDtypeStruct((N_SEQ_PAD, n), F32),
        compiler_params=_cparams(("arbitrary",)),
        name="adaln_mod",
    )(c_pad, w_ada, b_ada.reshape(1, n))


def _inproj_kernel(x_ref, sh_ref, sc_ref, g_ref, w_ref, qg_ref, kg_ref, o_ref, h_ref):
    i = pl.program_id(0)
    j = pl.program_id(1)
    slot = i % 2
    n_chunk = x_ref.shape[0] // INPROJ_ROW_CHUNK

    def norm_rows(rows, dst_slot):
        x = x_ref[rows, :]
        ms = jnp.mean(x * x, axis=-1, keepdims=True)
        y = x * lax.rsqrt(ms + NORM_EPS) * g_ref[...]
        h_ref[dst_slot, rows, :] = (y * (1.0 + sc_ref[...]) + sh_ref[...]).astype(BF16)

    @pl.when((i == 0) & (j == 0))
    def _():
        for r in range(n_chunk):
            norm_rows(pl.ds(r * INPROJ_ROW_CHUNK, INPROJ_ROW_CHUNK), slot)

    def head_norm(sub, g):
        ms = jnp.mean(sub * sub, axis=-1, keepdims=True)
        return sub * lax.rsqrt(ms + NORM_EPS) * g

    def chunked(epilogue, side=None):
        for r in range(n_chunk):
            rows = pl.ds(r * INPROJ_ROW_CHUNK, INPROJ_ROW_CHUNK)
            epilogue(rows, jnp.dot(h_ref[slot, rows, :], w_ref[...], preferred_element_type=F32))
            if side is not None:
                side(rows)

    @pl.when(j < Q_TILES[1])
    def _():
        g = qg_ref[...] * (HEAD_DIM ** -0.5 * LOG2E)

        def epilogue(rows, acc):
            for hh in range(COL_TILE // HEAD_DIM):
                cols = slice(hh * HEAD_DIM, (hh + 1) * HEAD_DIM)
                o_ref[rows, cols] = head_norm(acc[:, cols], g).astype(BF16)
        chunked(epilogue)

    @pl.when((j >= GELU_TILES[0]) & (j < GELU_TILES[1]))
    def _():
        def epilogue(rows, acc):
            o_ref[rows, :] = _gelu_tanh(acc).astype(BF16)
        chunked(epilogue)

    @pl.when((j >= SIG_TILES[0]) & (j < SIG_TILES[1]))
    def _():
        def epilogue(rows, acc):
            o_ref[rows, :] = _sigmoid(acc).astype(BF16)
        chunked(epilogue)

    @pl.when(j == KV_TILE)
    def _():
        g = kg_ref[...]

        def epilogue(rows, acc):
            for hh in range(N_KV_HEADS):
                cols = slice(hh * HEAD_DIM, (hh + 1) * HEAD_DIM)
                o_ref[rows, cols] = head_norm(acc[:, cols], g).astype(BF16)
            o_ref[rows, KV_WIDTH:] = acc[:, KV_WIDTH:].astype(BF16)
        chunked(epilogue, side=lambda rows: norm_rows(rows, 1 - slot))


def _in_projection(x2, mod3, norm_g, w_in_b, q_g, k_g, seq_len, tm):
    t = x2.shape[0]
    n_row_tiles = t // tm
    assert KV_TILE == N_COL_TILES - 1
    x_tile = lambda i, j: jnp.minimum(i + jnp.where(j == N_COL_TILES - 1, 1, 0), n_row_tiles - 1)
    seq = lambda i, j: (x_tile(i, j) * tm) // seq_len
    w_tile = lambda j: jnp.where(j < Q_TILES[1], j, jnp.where(j == KV_TILE, Q_TILES[1], j + 1))
    return pl.pallas_call(
        _inproj_kernel,
        grid=(n_row_tiles, N_COL_TILES),
        in_specs=[pl.BlockSpec((tm, D_MODEL), lambda i, j: (x_tile(i, j), 0)),
                  pl.BlockSpec((None, 1, D_MODEL), lambda i, j: (seq(i, j), 0, 0)),
                  pl.BlockSpec((None, 1, D_MODEL), lambda i, j: (seq(i, j), 0, 1)),
                  pl.BlockSpec((1, D_MODEL), lambda i, j: (0, 0)),
                  pl.BlockSpec((D_MODEL, COL_TILE), lambda i, j: (0, w_tile(j))),
                  pl.BlockSpec((1, HEAD_DIM), lambda i, j: (0, 0)),
                  pl.BlockSpec((1, HEAD_DIM), lambda i, j: (0, 0))],
        out_specs=pl.BlockSpec((tm, COL_TILE), lambda i, j: (i, j)),
        out_shape=jax.ShapeDtypeStruct((t, IN_WIDTH), BF16),
        scratch_shapes=[pltpu.VMEM((2, tm, D_MODEL), BF16)],
        compiler_params=_cparams(("arbitrary", "arbitrary")),
        name="norm_inproj",
    )(x2, mod3, mod3, norm_g, w_in_b, q_g, k_g)


def _alibi_slope(head):
    return 2.0 ** (-8.0 * (head + 1) / N_Q_HEADS)


def _attn_kernel(sink_ref, q_ref, kp_ref, kc_ref, kn_ref, vp_ref, vc_ref, vn_ref, o_ref, pen_ref, *,
                 steps_per_seq):
    n = pl.program_id(0)
    pos = n % steps_per_seq
    first = pos == 0
    last = pos == steps_per_seq - 1

    @pl.when(n == 0)
    def _():
        qi = lax.broadcasted_iota(I32, (BLOCK, 3 * BLOCK), 0)
        kj = lax.broadcasted_iota(I32, (BLOCK, 3 * BLOCK), 1)
        absrel = jnp.abs(kj - BLOCK - qi)
        in_window = absrel <= BLOCK
        absrel_f = absrel.astype(F32)
        masks = (in_window, in_window & (kj >= BLOCK), in_window & (kj < 2 * BLOCK))
        for v, valid in enumerate(masks):
            for hd in range(N_Q_HEADS):
                pen_ref[v, hd] = jnp.where(valid, (_alibi_slope(hd) * LOG2E) * absrel_f, jnp.inf)

    variants = []
    for b in range(ATTN_Q_BLOCKS):
        variant = 0
        if b == 0:
            variant = jnp.where(first, 1, variant)
        if b == ATTN_Q_BLOCKS - 1:
            variant = jnp.where(last, 2, variant)
        variants.append(variant)

    nt_dims = (((1,), (1,)), ((), ()))
    for h in range(N_KV_HEADS):
        kv_cols = slice(h * HEAD_DIM, (h + 1) * HEAD_DIM)
        k_parts = [kp_ref[:, kv_cols]] + [kc_ref[c * BLOCK:(c + 1) * BLOCK, kv_cols]
                                          for c in range(ATTN_Q_BLOCKS)] + [kn_ref[:, kv_cols]]
        v_parts = [vp_ref[:, kv_cols]] + [vc_ref[c * BLOCK:(c + 1) * BLOCK, kv_cols]
                                          for c in range(ATTN_Q_BLOCKS)] + [vn_ref[:, kv_cols]]
        heads = [h * Q_PER_KV + g for g in range(Q_PER_KV)]
        cols = [slice(hd * HEAD_DIM, (hd + 1) * HEAD_DIM) for hd in heads]
        sinks = [sink_ref[hd] * LOG2E for hd in heads]
        blocks = range(ATTN_Q_BLOCKS)
        q_rows = [slice(b * BLOCK, (b + 1) * BLOCK) for b in blocks]
        s4 = [lax.dot_general(jnp.concatenate([q_ref[q_rows[b], c] for c in cols], axis=0),
                              jnp.concatenate(k_parts[b:b + 3], axis=0), nt_dims, preferred_element_type=F32)
              for b in blocks]
        chains = [(b, g) for b in blocks for g in range(Q_PER_KV)]
        s = [s4[b][g * BLOCK:(g + 1) * BLOCK] - pen_ref[variants[b], heads[g]] for b, g in chains]
        m = [jnp.maximum(jnp.max(sc, axis=-1, keepdims=True), sinks[g]) for sc, (b, g) in zip(s, chains)]
        p = [jnp.exp2(sc - mc) for sc, mc in zip(s, m)]
        denom = [jnp.sum(pc, axis=-1, keepdims=True) + jnp.exp2(sinks[g] - mc)
                 for pc, mc, (b, g) in zip(p, m, chains)]
        o4 = [jnp.dot(jnp.concatenate([p[b * Q_PER_KV + g].astype(BF16) for g in range(Q_PER_KV)], axis=0),
                      jnp.concatenate(v_parts[b:b + 3], axis=0), preferred_element_type=F32) for b in blocks]
        for (b, g), dc in zip(chains, denom):
            o_ref[q_rows[b], cols[g]] = (o4[b][g * BLOCK:(g + 1) * BLOCK] / dc).astype(BF16)


def _attention(z, sink, seq_len):
    t = z.shape[0]
    nb = t // BLOCK
    qb = ATTN_Q_BLOCKS
    k_col = (IN_WIDTH - 2 * KV_WIDTH) // KV_WIDTH
    v_col = k_col + 1
    halo = lambda col: [pl.BlockSpec((BLOCK, KV_WIDTH), lambda n, s: (jnp.maximum(qb * n - 1, 0), col)),
                        pl.BlockSpec((qb * BLOCK, KV_WIDTH), lambda n, s: (n, col)),
                        pl.BlockSpec((BLOCK, KV_WIDTH), lambda n, s: (jnp.minimum(qb * n + qb, nb - 1), col))]
    grid_spec = pltpu.PrefetchScalarGridSpec(
        num_scalar_prefetch=1,
        grid=(nb // qb,),
        in_specs=[pl.BlockSpec((qb * BLOCK, D_MODEL), lambda n, s: (n, 0))] + halo(k_col) + halo(v_col),
        out_specs=pl.BlockSpec((qb * BLOCK, D_MODEL), lambda n, s: (n, 0)),
        scratch_shapes=[pltpu.VMEM((3, N_Q_HEADS, BLOCK, 3 * BLOCK), F32)],
    )
    assert qb >= 2
    return pl.pallas_call(
        functools.partial(_attn_kernel, steps_per_seq=seq_len // (qb * BLOCK)),
        grid_spec=grid_spec,
        out_shape=jax.ShapeDtypeStruct((t, D_MODEL), BF16),
        compiler_params=_cparams(("arbitrary",)),
        name="window_attn",
    )(sink, z, z, z, z, z, z, z)


def _post_kernel(attn_ref, u_ref, vg_ref, ga_ref, gs_ref, x_ref, g1_ref, sh2_ref, sc2_ref, n2g_ref,
                 lng_ref, lnb_ref, ws_ref, bs_ref, woa_ref, wos_ref, wout_ref, wr_ref,
                 x1_ref, h2p_ref, aff_ref, afft_ref, sgu_ref):
    step = pl.program_id(0)
    tm = x_ref.shape[0]

    def prepare(dst):
        vg = vg_ref[...].astype(F32)
        mu = jnp.mean(vg, axis=-1, keepdims=True)
        cen = vg - mu
        var = jnp.mean(cen * cen, axis=-1, keepdims=True)
        vn = (cen * lax.rsqrt(var + NORM_EPS) * lng_ref[...] + lnb_ref[...]).astype(BF16)
        for c in range(tm // BLOCK):
            rows = slice(c * BLOCK, (c + 1) * BLOCK)
            for g in range(N_SGU_GROUPS):
                cols = slice(g * BLOCK, (g + 1) * BLOCK)
                mixed = jnp.dot(ws_ref[g], vn[rows, cols], preferred_element_type=F32) + bs_ref[g]
                sgu_ref[dst, rows, cols] = (u_ref[rows, cols].astype(F32) * mixed).astype(BF16)

    @pl.when(step == 0)
    def _():
        prepare(0)

    @pl.when(step > 0)
    def _():
        prepare(step % 2)
        _post_finish(attn_ref, ga_ref, gs_ref, x_ref, g1_ref, sh2_ref, sc2_ref, n2g_ref, woa_ref, wos_ref,
                     wout_ref, wr_ref, x1_ref, h2p_ref, aff_ref, afft_ref, sgu_ref.at[(step + 1) % 2])


def _post_finish(attn_ref, ga_ref, gs_ref, x_ref, g1_ref, sh2_ref, sc2_ref, n2g_ref, woa_ref, wos_ref,
                 wout_ref, wr_ref, x1_ref, h2p_ref, aff_ref, afft_ref, sgu_ref):
    tm = x_ref.shape[0]
    a = jnp.dot(attn_ref[...], woa_ref[...], preferred_element_type=F32)
    s = jnp.dot(sgu_ref[...], wos_ref[...], preferred_element_type=F32)
    merged = (ga_ref[...].astype(F32) * a + gs_ref[...].astype(F32) * s).astype(BF16)
    mix = jnp.dot(merged, wout_ref[...], preferred_element_type=F32)
    x1 = x_ref[...] + g1_ref[...] * mix
    x1_ref[...] = x1
    ms = jnp.mean(x1 * x1, axis=-1, keepdims=True)
    h2 = x1 * lax.rsqrt(ms + NORM_EPS) * n2g_ref[...]
    h2 = h2 * (1.0 + sc2_ref[...]) + sh2_ref[...]
    h2b = h2.astype(BF16)
    half = D_MODEL // 2
    n_slab = half // LANES
    lo = lax.shift_right_logical(lax.bitcast_convert_type(h2b[:, :half].astype(F32), I32), 16)
    hi = lax.bitcast_convert_type(h2b[:, half:].astype(F32), I32) & jnp.int32(-65536)
    word = hi | lo
    for j in range(n_slab):
        h2p_ref[pl.ds(j, tm, stride=n_slab), :] = word[:, j * LANES:(j + 1) * LANES]
    h_lo = (h2 - h2b.astype(F32)).astype(BF16)
    r1 = jnp.dot(h2b, wr_ref[...], preferred_element_type=F32)
    r2 = jnp.dot(h_lo, wr_ref[:, :LANES], preferred_element_type=F32)
    logits = r1[:, :LANES] + r1[:, LANES:] + r2
    lane = lax.broadcasted_iota(I32, logits.shape, 1)
    logits = jnp.where(lane < N_EXPERTS, logits, -jnp.inf)
    logits = logits - jnp.max(logits, axis=-1, keepdims=True)
    ex = jnp.exp(logits)
    aff = ex / jnp.sum(ex, axis=-1, keepdims=True)
    aff_ref[...] = aff[:, :N_EXPERTS]
    afft_ref[...] = aff.T[:N_EXPERTS, :]


def _post_mixer(attn_o, z, x2, mod3, norm2_g, ln_g, ln_b, ws, bs, w_oa, w_os, w_out, w_r2, seq_len, tm):
    t = x2.shape[0]
    n_slab = D_MODEL // 2 // LANES
    n_tiles = t // tm
    ahead = lambda s: jnp.minimum(s, n_tiles - 1)
    done = lambda s: jnp.maximum(s - 1, 0)
    seq = lambda s: (done(s) * tm) // seq_len
    const2 = lambda s: (0, 0)
    const3 = lambda s: (0, 0, 0)
    resident = lambda shape, imap: pl.BlockSpec(shape, imap, pipeline_mode=pl.Buffered(1))
    tok = lambda col: pl.BlockSpec((tm, D_MODEL), lambda s: (done(s), col))
    tok_ahead = lambda col: pl.BlockSpec((tm, D_MODEL), lambda s: (ahead(s), col))
    modv = lambda col: pl.BlockSpec((None, 1, D_MODEL), lambda s: (seq(s), 0, col))
    return pl.pallas_call(
        _post_kernel,
        grid=(n_tiles + 1,),
        in_specs=[tok(0),
                  tok_ahead(1), tok_ahead(2),
                  tok(3), tok(4),
                  tok(0),
                  modv(2), modv(3), modv(4),
                  pl.BlockSpec((1, D_MODEL), const2),
                  pl.BlockSpec((1, D_MODEL), const2),
                  pl.BlockSpec((1, D_MODEL), const2),
                  resident((N_SGU_GROUPS, BLOCK, BLOCK), const3),
                  resident((N_SGU_GROUPS, BLOCK, BLOCK), const3),
                  resident((D_MODEL, D_MODEL), const2),
                  resident((D_MODEL, D_MODEL), const2),
                  resident((D_MODEL, D_MODEL), const2),
                  resident((D_MODEL, 2 * LANES), const2)],
        out_specs=[pl.BlockSpec((tm, D_MODEL), lambda s: (done(s), 0)),
                   pl.BlockSpec((tm * n_slab, LANES), lambda s: (done(s), 0)),
                   pl.BlockSpec((tm, N_EXPERTS), lambda s: (done(s), 0)),
                   pl.BlockSpec((N_EXPERTS, tm), lambda s: (0, done(s)))],
        out_shape=[jax.ShapeDtypeStruct((t, D_MODEL), F32),
                   jax.ShapeDtypeStruct((t * n_slab, LANES), I32),
                   jax.ShapeDtypeStruct((t, N_EXPERTS), F32),
                   jax.ShapeDtypeStruct((N_EXPERTS, t), F32)],
        scratch_shapes=[pltpu.VMEM((2, tm, D_MODEL), BF16)],
        compiler_params=_cparams(("arbitrary",)),
        name="post_mixer",
    )(attn_o, z, z, z, z, x2, mod3, mod3, mod3, norm2_g, ln_g, ln_b, ws, bs, w_oa, w_os, w_out, w_r2)


def _route_kernel(aff_ref, idx_ref, tau_ref, need_ref, pref_ref, eqpref_ref, bits_ref, taus_ref, *, cap):
    n_exp, n_rows, _ = aff_ref.shape
    n_tok = n_rows * LANES
    bits_ref[...] = lax.bitcast_convert_type(aff_ref[...], I32)

    def bisect(i, v):
        cand = v | lax.shift_left(jnp.int32(1), 30 - i)
        ge = _ones_where(bits_ref[...] >= cand)
        cnt = jnp.sum(jnp.sum(ge, axis=1, keepdims=True), axis=2, keepdims=True)
        return jnp.where(cnt >= cap, cand, v)

    taus_ref[...] = lax.fori_loop(0, 31, bisect, jnp.zeros((n_exp, 1, LANES), I32))

    li = lax.broadcasted_iota(I32, (LANES, LANES), 0)
    lj = lax.broadcasted_iota(I32, (LANES, LANES), 1)
    upper_incl = _ones_where(li <= lj, BF16)
    ones_sq = jnp.ones((LANES, LANES), BF16)
    ri = lax.broadcasted_iota(I32, (n_rows, n_rows), 0)
    rj = lax.broadcasted_iota(I32, (n_rows, n_rows), 1)
    lower_strict = _ones_where(rj < ri, BF16)
    upper_strict = _ones_where(ri < rj, BF16)
    ones_rows = jnp.ones((SUBLANES, LANES), BF16)
    nt_dims = (((1,), (1,)), ((), ()))

    def incl_cumsum(mb):
        local = jnp.dot(mb, upper_incl, preferred_element_type=F32)
        totb = jnp.dot(mb, ones_sq, preferred_element_type=F32)
        prefc = jnp.dot(lower_strict, totb.astype(BF16), preferred_element_type=F32)
        return local + prefc

    def row_prefix(mb):
        tot_row = lax.dot_general(ones_rows, mb, nt_dims, preferred_element_type=F32)
        pref_row = jnp.dot(tot_row.astype(BF16), upper_strict, preferred_element_type=F32)
        return tot_row, pref_row

    def per_expert(e, carry):
        t = taus_ref[e]
        b = bits_ref[e]
        gt = b > t
        eq = b == t
        eqb = _ones_where(eq, BF16)
        need = cap - jnp.sum(_ones_where(gt))
        sel = gt | (eq & (incl_cumsum(eqb) <= need))
        m = _ones_where(sel, BF16)
        glob = incl_cumsum(m)
        tot_row, pref_row = row_prefix(m)
        incl_row = pref_row + tot_row
        _, eq_pref_row = row_prefix(eqb)
        pref_ref[e] = pref_row.astype(I32)
        eqpref_ref[e] = eq_pref_row
        tau_ref[e] = jnp.broadcast_to(lax.bitcast_convert_type(t, F32), (SUBLANES, LANES))
        need_ref[e] = jnp.full((SUBLANES, LANES), need, F32)
        ghi = jnp.floor(glob * (1.0 / MXU_DIM))
        glo = (glob - MXU_DIM * ghi).astype(BF16)
        ghi = ghi.astype(BF16)
        pr = pref_row[0:1, :]
        ir = incl_row[0:1, :]

        n_chunk = cap // LANES
        group = ROUTE_CHUNK_GROUP if n_chunk % ROUTE_CHUNK_GROUP == 0 else 1

        def chunks(cg, carry2):
            cs = [cg * group + k for k in range(group)]
            s_r = [(c * LANES + lax.broadcasted_iota(I32, (LANES, n_rows), 0)).astype(F32) for c in cs]
            onehot = [_ones_where((pr <= s) & (s < ir), BF16) for s in s_r]
            rowid = [jnp.sum(_ones_where(ir <= s), axis=-1, keepdims=True) for s in s_r]
            grow = [MXU_DIM * jnp.dot(oh, ghi, preferred_element_type=F32)
                    + jnp.dot(oh, glo, preferred_element_type=F32) for oh in onehot]
            s_l = [(c * LANES + lax.broadcasted_iota(I32, (LANES, LANES), 0)).astype(F32) for c in cs]
            inrow = [jnp.sum(_ones_where(gr <= s), axis=-1, keepdims=True) for gr, s in zip(grow, s_l)]
            tok = [jnp.minimum(r * LANES + q, n_tok - 1.0) for r, q in zip(rowid, inrow)]
            tok_t = [jnp.broadcast_to(t_, (LANES, LANES)).T for t_ in tok]
            for c, tt in zip(cs, tok_t):
                idx_ref[e, pl.ds(c, 1), :] = tt[0:1, :].astype(I32)
            return carry2

        lax.fori_loop(0, n_chunk // group, chunks, 0)
        return carry

    lax.fori_loop(0, n_exp, per_expert, 0)


def _route(aff3, cap):
    n_exp, n_rows, _ = aff3.shape
    rep = lambda dt, w: jax.ShapeDtypeStruct((n_exp, SUBLANES, w), dt)
    return pl.pallas_call(
        functools.partial(_route_kernel, cap=cap),
        out_shape=[jax.ShapeDtypeStruct((n_exp, cap // LANES, LANES), I32),
                   rep(F32, LANES), rep(F32, LANES), rep(I32, n_rows), rep(F32, n_rows)],
        scratch_shapes=[pltpu.VMEM((n_exp, n_rows, LANES), I32), pltpu.VMEM((n_exp, 1, LANES), I32)],
        compiler_params=pltpu.CompilerParams(vmem_limit_bytes=VMEM_LIMIT),
        name="ec_route",
    )(aff3)


def _ffn_kernel(idx_ref, h2p_hbm, wg_hbm, wu_hbm, wd_hbm, o_ref, xraw_ref, xb_ref, acc_ref, wgb_ref, wub_ref,
                wdb_ref, sem, wsem, *, tc, rows_per_step, slab, fc, nf):
    ct_n = pl.num_programs(1)
    expert = pl.program_id(0)
    tile = expert * ct_n + pl.program_id(1)
    n_tiles = pl.num_programs(0) * ct_n
    slot = tile % 2
    nxt = jnp.minimum(tile + 1, n_tiles - 1)
    w_parity = (tile * nf) % 2

    def row_copy(tile_id, s, dst_slot):
        tok = idx_ref[tile_id * tc + jnp.minimum(s, tc - 1)]
        return pltpu.make_async_copy(
            h2p_hbm.at[pl.ds(pl.multiple_of(tok * slab, slab), slab), :],
            xraw_ref.at[dst_slot, pl.ds(pl.multiple_of(s * slab, slab), slab), :], sem.at[dst_slot])

    def wait_slot(s_):
        pltpu.make_async_copy(xraw_ref.at[s_], xraw_ref.at[s_], sem.at[s_]).wait()

    def weight_copies(e, f, ws):
        cols = pl.ds(pl.multiple_of(f * fc, fc), fc)
        return (pltpu.make_async_copy(wg_hbm.at[e, :, cols], wgb_ref.at[ws], wsem.at[ws]),
                pltpu.make_async_copy(wu_hbm.at[e, :, cols], wub_ref.at[ws], wsem.at[ws]),
                pltpu.make_async_copy(wd_hbm.at[e, cols, :], wdb_ref.at[ws], wsem.at[ws]))

    @pl.when(tile == 0)
    def _():
        def body(s, carry):
            row_copy(tile, s, slot).start()
            return carry
        lax.fori_loop(0, rows_per_step * nf, body, 0)
        for c in weight_copies(expert, 0, w_parity):
            c.start()

    wait_slot(slot)
    half = slab * LANES
    for j in range(slab):
        w = xraw_ref[slot, pl.ds(j, tc, stride=slab), :]
        lo = lax.bitcast_convert_type(lax.shift_left(w, 16), F32)
        hi = lax.bitcast_convert_type(w & jnp.int32(-65536), F32)
        xb_ref[:, j * LANES:(j + 1) * LANES] = lo.astype(BF16)
        xb_ref[:, half + j * LANES:half + (j + 1) * LANES] = hi.astype(BF16)
    def chunk(f, is_first=False, is_last=False):
        ws = (w_parity + f) % 2
        for c in weight_copies(expert, f, ws):
            c.wait()
        if is_last:
            @pl.when(tile < n_tiles - 1)
            def _():
                for c in weight_copies(nxt // ct_n, 0, 1 - ws):
                    c.start()
        else:
            for c in weight_copies(expert, f + 1, 1 - ws):
                c.start()

        for u in range(rows_per_step):
            row_copy(nxt, f * rows_per_step + u, 1 - slot).start()

        x = xb_ref[...]
        g = jnp.dot(x, wgb_ref[ws].astype(BF16), preferred_element_type=F32)
        up = jnp.dot(x, wub_ref[ws].astype(BF16), preferred_element_type=F32)
        hmid = (g * _sigmoid(g) * up).astype(BF16)
        part = jnp.dot(hmid, wdb_ref[ws].astype(BF16), preferred_element_type=F32)
        if is_first:
            acc_ref[...] = part
        elif is_last:
            o_ref[...] = (acc_ref[...] + part).astype(BF16)
        else:
            acc_ref[...] += part

    def middle(f, carry):
        chunk(f)
        return carry

    chunk(0, is_first=True)
    lax.fori_loop(1, nf - 1, middle, 0)
    chunk(nf - 1, is_last=True)

    @pl.when(tile == n_tiles - 1)
    def _():
        wait_slot(1 - slot)


def _expert_ffn(idx_flat, h2p, w_gate, w_up, w_down, cap, tc, fc):
    n_exp, d, d_ff = w_gate.shape
    slab = d // 2 // LANES
    nct = cap // tc
    nf = d_ff // fc
    assert nf >= 2
    rows_per_step = -(-tc // nf)
    any_spec = pl.BlockSpec(memory_space=pl.ANY)
    grid_spec = pltpu.PrefetchScalarGridSpec(
        num_scalar_prefetch=1,
        grid=(n_exp, nct),
        in_specs=[any_spec, any_spec, any_spec, any_spec],
        out_specs=pl.BlockSpec((tc, d), lambda ei, ci, idx: (ei * nct + ci, 0)),
        scratch_shapes=[pltpu.VMEM((2, rows_per_step * nf * slab, LANES), I32),
                        pltpu.VMEM((tc, d), BF16),
                        pltpu.VMEM((tc, d), F32),
                        pltpu.VMEM((2, d, fc), w_gate.dtype),
                        pltpu.VMEM((2, d, fc), w_up.dtype),
                        pltpu.VMEM((2, fc, d), w_down.dtype),
                        pltpu.SemaphoreType.DMA((2,)),
                        pltpu.SemaphoreType.DMA((2,))],
    )
    return pl.pallas_call(
        functools.partial(_ffn_kernel, tc=tc, rows_per_step=rows_per_step, slab=slab, fc=fc, nf=nf),
        grid_spec=grid_spec,
        out_shape=jax.ShapeDtypeStruct((n_exp * cap, d), BF16),
        compiler_params=_cparams(("arbitrary", "arbitrary")),
        name="expert_ffn",
    )(idx_flat, h2p, w_gate, w_up, w_down)


def _combine_kernel(tab_ref, aff_ref, tau_ref, need_ref, eqs_ref, x1_ref, g2_ref, y_hbm, o_ref,
                    ybuf_ref, sem, *, cap, n_tile):
    i = pl.program_id(0)
    tm, n_exp = aff_ref.shape
    slot_rows = COMBINE_SLOT
    k_rows = n_exp * slot_rows
    total_rows = n_exp * cap
    lane_e = lax.broadcasted_iota(I32, (1, n_exp), 1)

    def geometry(tile, e):
        n0 = tab_ref[e * (n_tile + 1) + tile]
        n1 = tab_ref[e * (n_tile + 1) + tile + 1]
        first = e * cap + n0
        aligned = (first // BF16_ROWS) * BF16_ROWS
        return aligned, first - aligned, n1 - n0

    def round_src(aligned, q):
        src = aligned + q * slot_rows
        clamped = jnp.minimum(src, total_rows - slot_rows)
        return clamped, src - clamped

    def issue_round(tile, q, buf):
        for e in range(n_exp):
            aligned, _, _ = geometry(tile, e)
            src, _ = round_src(aligned, q)
            pltpu.make_async_copy(y_hbm.at[pl.ds(pl.multiple_of(src, BF16_ROWS), slot_rows), :],
                                  ybuf_ref.at[buf, pl.ds(e * slot_rows, slot_rows), :], sem.at[buf]).start()

    buf = i % 2

    @pl.when(i == 0)
    def _():
        issue_round(i, 0, buf)

    @pl.when(i + 1 < n_tile)
    def _():
        issue_round(i + 1, 0, 1 - buf)

    a = aff_ref[...]
    tau = tau_ref[...]
    eq = a == tau
    ti = lax.broadcasted_iota(I32, (tm, tm), 0)
    tj = lax.broadcasted_iota(I32, (tm, tm), 1)
    eq_rank = eqs_ref[...] + jnp.dot(_ones_where(tj <= ti, BF16), _ones_where(eq, BF16),
                                     preferred_element_type=F32)
    sel = (a > tau) | (eq & (eq_rank <= need_ref[...]))
    wm = jnp.where(sel, a, 0.0).astype(BF16)
    rank = jnp.dot(_ones_where(tj < ti, BF16), _ones_where(sel, BF16), preferred_element_type=F32)

    n_round = jnp.int32(1)
    for e in range(n_exp):
        _, delta, n_sel = geometry(i, e)
        n_round = jnp.maximum(n_round, (delta + n_sel + slot_rows - 1) // slot_rows)

    spread = _ones_where(lax.broadcasted_iota(I32, (n_exp, k_rows), 1) // slot_rows
                         == lax.broadcasted_iota(I32, (n_exp, k_rows), 0), BF16)
    col_in_slot = (lax.broadcasted_iota(I32, (tm, k_rows), 1) % slot_rows).astype(F32)
    wm_cols = jnp.dot(wm, spread, preferred_element_type=F32)

    def round_sum(q):
        pltpu.make_async_copy(ybuf_ref.at[buf], ybuf_ref.at[buf], sem.at[buf]).wait()
        offset = jnp.zeros((1, n_exp), F32)
        shift = jnp.zeros((1, n_exp), F32)
        for e in range(n_exp):
            aligned, delta, _ = geometry(i, e)
            _, sh = round_src(aligned, q)
            offset = jnp.where(lane_e == e, (delta - q * slot_rows).astype(F32), offset)
            shift = jnp.where(lane_e == e, sh.astype(F32), shift)
        u = rank + offset
        pos = jnp.where(u >= 0, u + shift, -1.0).astype(BF16)
        pos_cols = jnp.dot(pos, spread, preferred_element_type=F32)
        place = jnp.where(pos_cols == col_in_slot, wm_cols, 0.0).astype(BF16)
        return jnp.dot(place, ybuf_ref[buf], preferred_element_type=F32)

    o_ref[...] = x1_ref[...] + g2_ref[...] * round_sum(0)

    def extra_round(q, carry):
        issue_round(i, q, buf)
        o_ref[...] += g2_ref[...] * round_sum(q)
        return carry

    lax.fori_loop(1, n_round, extra_round, 0)


def _combine(tab, aff, tau, need, eqs, x1, mod3, y, cap, seq_len, tm):
    t, d = x1.shape
    n_exp = aff.shape[1]
    n_tile = t // tm
    seq = lambda i, tab_: ((i * tm) // seq_len, 0, 5)
    grid_spec = pltpu.PrefetchScalarGridSpec(
        num_scalar_prefetch=1,
        grid=(n_tile,),
        in_specs=[pl.BlockSpec((tm, n_exp), lambda i, tab_: (i, 0)),
                  pl.BlockSpec((1, n_exp), lambda i, tab_: (0, 0)),
                  pl.BlockSpec((1, n_exp), lambda i, tab_: (0, 0)),
                  pl.BlockSpec((None, 1, n_exp), lambda i, tab_: (i, 0, 0)),
                  pl.BlockSpec((tm, d), lambda i, tab_: (i, 0)),
                  pl.BlockSpec((None, 1, d), seq),
                  pl.BlockSpec(memory_space=pl.ANY)],
        out_specs=pl.BlockSpec((tm, d), lambda i, tab_: (i, 0)),
        scratch_shapes=[pltpu.VMEM((2, n_exp * COMBINE_SLOT, d), BF16),
                        pltpu.SemaphoreType.DMA((2,))],
    )
    return pl.pallas_call(
        functools.partial(_combine_kernel, cap=cap, n_tile=n_tile),
        grid_spec=grid_spec,
        out_shape=jax.ShapeDtypeStruct((t, d), F32),
        compiler_params=_cparams(("arbitrary",)),
        name="ec_combine",
    )(tab, aff, tau, need, eqs, x1, mod3, y)


def _moe(h2p, aff, afft, x1, mod_g, w_gate, w_up, w_down, seq_len):
    t = aff.shape[0]
    n_exp = aff.shape[1]
    cap = CAPACITY_FACTOR * t // n_exp
    tm = COMBINE_TILE
    idx, tau, need, pref, eqpref = _route(afft.reshape(n_exp, t // LANES, LANES), cap)
    y = _expert_ffn(idx.reshape(-1), h2p, w_gate, w_up, w_down, cap, min(1024, cap), 256)
    rows_per_tile = tm // LANES
    tab = jnp.concatenate([pref[:, 0, ::rows_per_tile], jnp.full((n_exp, 1), cap, I32)], axis=1).reshape(-1)
    eqs = eqpref[:, 0, ::rows_per_tile].T.reshape(t // tm, 1, n_exp)
    return _combine(tab, aff, tau[:, 0, 0].reshape(1, n_exp), need[:, 0, 0].reshape(1, n_exp), eqs,
                    x1, mod_g, y, cap, seq_len, tm)


def _trunk(x, mod3, seq_base, prm):
    b, s, d = x.shape
    t = b * s
    x2 = x.reshape(t, d)
    mod_g = lax.slice_in_dim(mod3, seq_base, seq_base + b, axis=0)
    z = _in_projection(x2, mod_g, prm["norm1_g"], prm["w_in"], prm["q_g"], prm["k_g"], s, min(1024, s))
    attn_o = _attention(z, prm["sink"], s)
    x1, h2p, aff, afft = _post_mixer(attn_o, z, x2, mod_g, prm["norm2_g"], prm["ln_g"], prm["ln_b"], prm["ws"],
                                     prm["bs"], prm["w_oa"], prm["w_os"], prm["w_out"], prm["w_r2"], s, 256)
    out = _moe(h2p, aff, afft, x1, mod_g, prm["w_gate"], prm["w_up"], prm["w_down"], s)
    return out.reshape(b, s, d)


def kernel(x_prompt, x_sample, c_prompt, c_sample, w_ada, b_ada, norm1_g, norm2_g, w_in, q_norm_g, k_norm_g,
           attn_sink, sgu_ln_g, sgu_ln_b, sgu_w, sgu_b, w_o_attn, w_o_sgu, w_out, w_router, w_gate, w_up,
           w_down):
    assert w_ada.shape[0] == 1
    bp, bs_ = x_prompt.shape[0], x_sample.shape[0]
    assert bp + bs_ <= N_SEQ_PAD
    d = D_MODEL
    c_pad = jnp.zeros((N_SEQ_PAD, d), F32).at[:bp].set(c_prompt).at[bp:bp + bs_].set(c_sample)
    mod = _modulation(c_pad, w_ada[0], b_ada[0])
    mod3 = mod.reshape(N_SEQ_PAD, 1, 6 * d)

    w_r = w_router[0]
    w_r_hi = w_r.astype(BF16)
    w_r_lo = (w_r - w_r_hi.astype(F32)).astype(BF16)
    pad = ((0, 0), (0, LANES - N_EXPERTS))
    w_r2 = jnp.concatenate([jnp.pad(w_r_hi, pad), jnp.pad(w_r_lo, pad)], axis=1)
    prm = dict(
        norm1_g=norm1_g[0].reshape(1, d), norm2_g=norm2_g[0].reshape(1, d), w_in=w_in[0].astype(BF16),
        q_g=q_norm_g[0].reshape(1, HEAD_DIM), k_g=k_norm_g[0].reshape(1, HEAD_DIM), sink=attn_sink[0],
        ln_g=sgu_ln_g[0].reshape(1, d), ln_b=sgu_ln_b[0].reshape(1, d),
        ws=sgu_w[0].astype(BF16),
        bs=jnp.broadcast_to(sgu_b[0][:, :, None], (N_SGU_GROUPS, BLOCK, BLOCK)),
        w_oa=w_o_attn[0].astype(BF16), w_os=w_o_sgu[0].astype(BF16), w_out=w_out[0].astype(BF16),
        w_r2=w_r2, w_gate=w_gate[0], w_up=w_up[0], w_down=w_down[0])
    y_prompt = _trunk(x_prompt, mod3, 0, prm)
    y_sample = _trunk(x_sample, mod3, bp, prm)
    return (y_prompt, y_sample)
```
